```python
import math
import jax, jax.numpy as jnp
from jax import lax
import numpy as np

D_MODEL = 1024
BATCH = 8
SEQ = 8192
DEPTH = 4

N_MIXERS = 3
D_FF = 2816
CONV_WIDTH = 3
POOL_WINDOWS = (2, 4, 8, 16)
POOL_GROUPS = len(POOL_WINDOWS)
POOL_CH = D_MODEL // POOL_GROUPS
HEAD_DIM = 64
N_HEADS = D_MODEL // HEAD_DIM
N_KV_HEADS = 2
GQA_GROUP = N_HEADS // N_KV_HEADS
WINDOW = 128
BLOCK = 128
NUM_BUCKETS = 32
MAX_EXACT = NUM_BUCKETS // 2
MAX_DISTANCE = 128
EPS = 1e-6

N_A = (DEPTH + 2) // 3
N_B = (DEPTH + 1) // 3
N_C = DEPTH // 3

kernel_name = "interleaved_hybrid_conv_pool_swa_macaron"


def rmsnorm(x, g):
    xf = x.astype(jnp.float32)
    xf = xf * lax.rsqrt(jnp.mean(xf * xf, axis=-1, keepdims=True) + EPS)
    return (xf * g.astype(jnp.float32)).astype(x.dtype)


def swiglu(h, w_in, w_out):
    gate, up = jnp.split(h @ w_in, 2, axis=-1)
    return (jax.nn.silu(gate) * up) @ w_out


def short_conv_mixer(h, w_in, conv_w, w_out):
    b, c, v = jnp.split(h @ w_in, 3, axis=-1)
    z = c * v
    zp = jnp.pad(z, ((0, 0), (CONV_WIDTH - 1, 0), (0, 0)))
    s = z.shape[1]
    conv = conv_w[0] * zp[:, 0:s] + conv_w[1] * zp[:, 1:s + 1] + conv_w[2] * zp[:, 2:s + 2]
    return (b * conv) @ w_out


def multiscale_pool_mixer(h, w, bias, scale):
    bsz, s, d = h.shape
    hf = h.astype(jnp.float32).reshape(bsz, s, POOL_GROUPS, POOL_CH)
    cs = jnp.cumsum(hf, axis=1)
    t = jnp.arange(s)
    outs = []
    for g, win in enumerate(POOL_WINDOWS):
        c = cs[:, :, g]
        lower = jnp.pad(c, ((0, 0), (win, 0), (0, 0)))[:, :s]
        cnt = jnp.minimum(t + 1, win).astype(jnp.float32)[None, :, None]
        outs.append((c - lower) / cnt - hf[:, :, g])
    diff = jnp.stack(outs, axis=2).astype(h.dtype)
    y = jnp.einsum('bsgc,gcd->bsgd', diff, w).reshape(bsz, s, d) + bias
    return y * scale


def t5_causal_buckets(n):
    nf = np.maximum(n, 1).astype(np.float32)
    large = MAX_EXACT + (np.log(nf / MAX_EXACT) / math.log(MAX_DISTANCE / MAX_EXACT)
                         * (NUM_BUCKETS - MAX_EXACT)).astype(np.int32)
    large = np.minimum(large, NUM_BUCKETS - 1)
    return np.where(n < MAX_EXACT, n, large).astype(np.int32)


def head_rms(x, g):
    xf = x.astype(jnp.float32)
    xf = xf * lax.rsqrt(jnp.mean(xf * xf, axis=-1, keepdims=True) + EPS)
    return (xf * g.astype(jnp.float32)).astype(x.dtype)


def swa_attention_mixer(h, w_qkv, b_qkv, q_gain, k_gain, sinks, w_o, b_o, rel_bias):
    bsz, s, _ = h.shape
    nblk = s // BLOCK
    qkv = h @ w_qkv + b_qkv
    q, k, v = jnp.split(qkv, [N_HEADS * HEAD_DIM, (N_HEADS + N_KV_HEADS) * HEAD_DIM], axis=-1)
    q = head_rms(q.reshape(bsz, s, N_KV_HEADS, GQA_GROUP, HEAD_DIM), q_gain)
    k = head_rms(k.reshape(bsz, s, N_KV_HEADS, HEAD_DIM), k_gain)
    v = v.reshape(bsz, s, N_KV_HEADS, HEAD_DIM)

    q = q.reshape(bsz, nblk, BLOCK, N_KV_HEADS, GQA_GROUP, HEAD_DIM)

    def band(t):
        tb = t.reshape(bsz, nblk, BLOCK, N_KV_HEADS, HEAD_DIM)
        prev = jnp.concatenate([jnp.zeros_like(tb[:, :1]), tb[:, :-1]], axis=1)
        return jnp.concatenate([prev, tb], axis=2)

    kw, vw = band(k), band(v)
    scores = jnp.einsum('bnqhgd,bnkhd->bnhgqk', q, kw).astype(jnp.float32) * (HEAD_DIM ** -0.5)

    qi = np.arange(BLOCK)[:, None]
    ki = np.arange(2 * BLOCK)[None, :]
    dist = qi + BLOCK - ki
    buckets = t5_causal_buckets(dist)
    bias = rel_bias.astype(jnp.float32)[buckets]
    bias = jnp.transpose(bias, (2, 0, 1)).reshape(N_KV_HEADS, GQA_GROUP, BLOCK, 2 * BLOCK)
    scores = scores + bias[None, None]

    in_band = (dist >= 0) & (dist < WINDOW)
    key_pos = jnp.arange(nblk)[:, None, None] * BLOCK - BLOCK + jnp.asarray(ki)[None]
    mask = jnp.asarray(in_band)[None] & (key_pos >= 0)
    scores = jnp.where(mask[None, :, None, None], scores, -jnp.inf)

    sink = sinks.astype(jnp.float32).reshape(N_KV_HEADS, GQA_GROUP)[None, None, :, :, None, None]
    m = jnp.maximum(jnp.max(scores, axis=-1, keepdims=True), sink)
    p = jnp.exp(scores - m)
    denom = jnp.sum(p, axis=-1, keepdims=True) + jnp.exp(sink - m)
    probs = (p / denom).astype(vw.dtype)
    o = jnp.einsum('bnhgqk,bnkhd->bnqhgd', probs, vw).reshape(bsz, s, N_HEADS * HEAD_DIM)
    return o @ w_o + b_o


def setup_inputs(seed: int = 0) -> dict:
    key = jax.random.key(seed)
    ks = iter(jax.random.split(key, 32))

    def nrm(shape, scale):
        return jax.random.normal(next(ks), shape, jnp.float32) * scale

    def gain(shape):
        return 1.0 + nrm(shape, 0.1)

    qkv_w = (N_HEADS + 2 * N_KV_HEADS) * HEAD_DIM
    return {
        "x": nrm((BATCH, SEQ, D_MODEL), 1.0),
        "ffn1_norm": gain((DEPTH, D_MODEL)),
        "ffn1_w_in": nrm((DEPTH, D_MODEL, 2 * D_FF), D_MODEL ** -0.5),
        "ffn1_w_out": nrm((DEPTH, D_FF, D_MODEL), D_FF ** -0.5),
        "mix_norm": gain((DEPTH, D_MODEL)),
        "ffn2_norm": gain((DEPTH, D_MODEL)),
        "ffn2_w_in": nrm((DEPTH, D_MODEL, 2 * D_FF), D_MODEL ** -0.5),
        "ffn2_w_out": nrm((DEPTH, D_FF, D_MODEL), D_FF ** -0.5),
        "conv_w_in": nrm((N_A, D_MODEL, 3 * D_MODEL), D_MODEL ** -0.5),
        "conv_w": nrm((N_A, CONV_WIDTH, D_MODEL), CONV_WIDTH ** -0.5),
        "conv_w_out": nrm((N_A, D_MODEL, D_MODEL), D_MODEL ** -0.5),
        "pool_w": nrm((N_B, POOL_GROUPS, POOL_CH, POOL_CH), POOL_CH ** -0.5),
        "pool_b": nrm((N_B, D_MODEL), 0.02),
        "pool_scale": 0.5 + nrm((N_B, D_MODEL), 0.05),
        "attn_w_qkv": nrm((N_C, D_MODEL, qkv_w), D_MODEL ** -0.5),
        "attn_b_qkv": nrm((N_C, qkv_w), 0.02),
        "attn_q_norm": gain((N_C, HEAD_DIM)),
        "attn_k_norm": gain((N_C, HEAD_DIM)),
        "attn_sinks": nrm((N_C, N_HEADS), 0.5),
        "attn_w_o": nrm((N_C, N_HEADS * HEAD_DIM, D_MODEL), (N_HEADS * HEAD_DIM) ** -0.5),
        "attn_b_o": nrm((N_C, D_MODEL), 0.02),
        "rel_bias": nrm((NUM_BUCKETS, N_HEADS), 0.5),
    }


def reference(x, ffn1_norm, ffn1_w_in, ffn1_w_out, mix_norm, ffn2_norm, ffn2_w_in, ffn2_w_out,
              conv_w_in, conv_w, conv_w_out, pool_w, pool_b, pool_scale,
              attn_w_qkv, attn_b_qkv, attn_q_norm, attn_k_norm, attn_sinks, attn_w_o, attn_b_o,
              rel_bias):
    for i in range(DEPTH):
        x = x + 0.5 * swiglu(rmsnorm(x, ffn1_norm[i]), ffn1_w_in[i], ffn1_w_out[i])
        h = rmsnorm(x, mix_norm[i])
        kind, j = i % N_MIXERS, i // N_MIXERS
        if kind == 0:
            y = short_conv_mixer(h, conv_w_in[j], conv_w[j], conv_w_out[j])
        elif kind == 1:
            y = multiscale_pool_mixer(h, pool_w[j], pool_b[j], pool_scale[j])
        else:
            y = swa_attention_mixer(h, attn_w_qkv[j], attn_b_qkv[j], attn_q_norm[j], attn_k_norm[j],
                                    attn_sinks[j], attn_w_o[j], attn_b_o[j], rel_bias)
        x = x + y
        x = x + 0.5 * swiglu(rmsnorm(x, ffn2_norm[i]), ffn2_w_in[i], ffn2_w_out[i])
    return x
```

```python
import functools
import math

import numpy as np
import jax
import jax.numpy as jnp
from jax import lax
from jax.experimental import pallas as pl
from jax.experimental.pallas import tpu as pltpu

D_MODEL = 1024
D_FF = 2816
DEPTH = 4
N_MIXERS = 3
CONV_WIDTH = 3
POOL_WINDOWS = (2, 4, 8, 16)
POOL_CH = D_MODEL // len(POOL_WINDOWS)
HEAD_DIM = 64
N_HEADS = D_MODEL // HEAD_DIM
N_KV_HEADS = 2
GQA_GROUP = N_HEADS // N_KV_HEADS
WINDOW = 128
BLOCK = 128
NUM_BUCKETS = 32
MAX_EXACT = NUM_BUCKETS // 2
MAX_DISTANCE = 128
EPS = 1e-6

LANES = 128
SUBLANES = 8
MASK_VALUE = -1e30
VMEM_LIMIT_BYTES = 56 * 1024 * 1024

FFN_TOKENS = 512
FFN_CHUNK = 256
MIX_TOKENS = 512
POOL_HALO = 16
HEAD_PAIRS = N_HEADS // 2
PAIRS_PER_KV = HEAD_PAIRS // N_KV_HEADS

F32 = jnp.float32
BF16 = jnp.bfloat16


def _rmsnorm_bf16(x, g):
    ms = jnp.mean(x * x, axis=-1, keepdims=True)
    return (x * lax.rsqrt(ms + EPS) * g).astype(BF16)


def _dot(a, b):
    return jnp.dot(a, b, preferred_element_type=F32)


def _const_spec(shape):
    return pl.BlockSpec(shape, lambda *_: (0,) * len(shape), pipeline_mode=pl.Buffered(1))


def _compiler_params(n_grid_dims):
    return pltpu.CompilerParams(
        dimension_semantics=("arbitrary",) * n_grid_dims,
        vmem_limit_bytes=VMEM_LIMIT_BYTES,
    )


def _ffn_body(x_ref, g_ref, win_ref, wout_ref, o_ref):
    x = x_ref[...]
    h = _rmsnorm_bf16(x, g_ref[...])
    acc = None
    for c in range(D_FF // FFN_CHUNK):
        lo = c * FFN_CHUNK
        gate = _dot(h, win_ref[:, lo:lo + FFN_CHUNK])
        up = _dot(h, win_ref[:, D_FF + lo:D_FF + lo + FFN_CHUNK])
        act = (gate * jax.nn.sigmoid(gate) * up).astype(BF16)
        part = _dot(act, wout_ref[lo:lo + FFN_CHUNK, :])
        acc = part if acc is None else acc + part
    o_ref[...] = x + 0.5 * acc


def _ffn(x2, g, w_in, w_out):
    t = x2.shape[0]
    assert t % FFN_TOKENS == 0
    return pl.pallas_call(
        _ffn_body,
        grid=(t // FFN_TOKENS,),
        in_specs=[
            pl.BlockSpec((FFN_TOKENS, D_MODEL), lambda i: (i, 0)),
            _const_spec((1, D_MODEL)),
            _const_spec((D_MODEL, 2 * D_FF)),
            _const_spec((D_FF, D_MODEL)),
        ],
        out_specs=pl.BlockSpec((FFN_TOKENS, D_MODEL), lambda i: (i, 0)),
        out_shape=jax.ShapeDtypeStruct((t, D_MODEL), F32),
        compiler_params=_compiler_params(1),
        name="ffn",
    )(x2, g.reshape(1, D_MODEL), w_in.astype(BF16), w_out.astype(BF16))


def _conv_body(x_ref, g_ref, win_ref, cw_ref, wout_ref, o_ref, prev_ref):
    ts = x_ref.shape[0]

    @pl.when(pl.program_id(1) == 0)
    def _():
        prev_ref[...] = jnp.zeros_like(prev_ref)

    x = x_ref[...]
    h = _rmsnorm_bf16(x, g_ref[...])
    c = _dot(h, win_ref[:, D_MODEL:2 * D_MODEL])
    v = _dot(h, win_ref[:, 2 * D_MODEL:3 * D_MODEL])
    z = c * v
    prev = prev_ref[...]
    prev_ref[...] = z[ts - SUBLANES:, :]

    w0 = cw_ref[0:1, :]
    w1 = cw_ref[1:2, :]
    w2 = cw_ref[2:3, :]
    z1 = pltpu.roll(z, 1, 0)
    z2 = pltpu.roll(z, 2, 0)
    row = lax.broadcasted_iota(jnp.int32, (SUBLANES, D_MODEL), 0)
    head1 = jnp.where(row < 1, pltpu.roll(prev, 1, 0), z1[:SUBLANES])
    head2 = jnp.where(row < 2, pltpu.roll(prev, 2, 0), z2[:SUBLANES])
    conv_head = w0 * head2 + w1 * head1 + w2 * z[:SUBLANES]
    conv_rest = w0 * z2[SUBLANES:] + w1 * z1[SUBLANES:] + w2 * z[SUBLANES:]
    conv = jnp.concatenate([conv_head, conv_rest], axis=0)

    b = _dot(h, win_ref[:, 0:D_MODEL])
    u = (b * conv).astype(BF16)
    o_ref[...] = x + _dot(u, wout_ref[...])


def _conv_mixer(x, g, w_in, conv_w, w_out):
    bsz, s, _ = x.shape
    ts = MIX_TOKENS
    assert s % ts == 0
    return pl.pallas_call(
        _conv_body,
        grid=(bsz, s // ts),
        in_specs=[
            pl.BlockSpec((None, ts, D_MODEL), lambda b, i: (b, i, 0)),
            _const_spec((1, D_MODEL)),
            _const_spec((D_MODEL, 3 * D_MODEL)),
            _const_spec((CONV_WIDTH, D_MODEL)),
            _const_spec((D_MODEL, D_MODEL)),
        ],
        out_specs=pl.BlockSpec((None, ts, D_MODEL), lambda b, i: (b, i, 0)),
        out_shape=jax.ShapeDtypeStruct(x.shape, F32),
        scratch_shapes=[pltpu.VMEM((SUBLANES, D_MODEL), F32)],
        compiler_params=_compiler_params(2),
        name="conv_mixer",
    )(x, g.reshape(1, D_MODEL), w_in.astype(BF16), conv_w, w_out.astype(BF16))


def _pool_body(x_ref, g_ref, w_ref, b_ref, sc_ref, o_ref, prev_ref):
    ts = x_ref.shape[0]
    i = pl.program_id(1)

    @pl.when(i == 0)
    def _():
        prev_ref[...] = jnp.zeros_like(prev_ref)

    x = x_ref[...]
    ms = jnp.mean(x * x, axis=-1, keepdims=True)
    h = x * lax.rsqrt(ms + EPS) * g_ref[...]
    prev = prev_ref[...]
    prev_ref[...] = h[ts - POOL_HALO:, :]

    pos = i * ts + lax.broadcasted_iota(jnp.int32, (ts, POOL_CH), 0)
    ys = []
    for gi, win in enumerate(POOL_WINDOWS):
        lo = gi * POOL_CH
        hg = h[:, lo:lo + POOL_CH]
        ssum = jnp.concatenate([prev[:, lo:lo + POOL_CH], hg], axis=0)
        shift = 1
        while shift < win:
            ssum = ssum + pltpu.roll(ssum, shift, 0)
            shift *= 2
        cnt = jnp.minimum(pos + 1, win).astype(F32)
        diff = ssum[POOL_HALO:] / cnt - hg
        ys.append(_dot(diff.astype(BF16), w_ref[gi]))
    y = jnp.concatenate(ys, axis=1)
    o_ref[...] = x + (y + b_ref[...]) * sc_ref[...]


def _pool_mixer(x, g, w, bias, scale):
    bsz, s, _ = x.shape
    ts = MIX_TOKENS
    assert s % ts == 0
    ng = len(POOL_WINDOWS)
    return pl.pallas_call(
        _pool_body,
        grid=(bsz, s // ts),
        in_specs=[
            pl.BlockSpec((None, ts, D_MODEL), lambda b, i: (b, i, 0)),
            _const_spec((1, D_MODEL)),
            _const_spec((ng, POOL_CH, POOL_CH)),
            _const_spec((1, D_MODEL)),
            _const_spec((1, D_MODEL)),
        ],
        out_specs=pl.BlockSpec((None, ts, D_MODEL), lambda b, i: (b, i, 0)),
        out_shape=jax.ShapeDtypeStruct(x.shape, F32),
        scratch_shapes=[pltpu.VMEM((POOL_HALO, D_MODEL), F32)],
        compiler_params=_compiler_params(2),
        name="pool_mixer",
    )(x, g.reshape(1, D_MODEL), w.astype(BF16), bias.reshape(1, D_MODEL), scale.reshape(1, D_MODEL))


def _t5_causal_buckets(n):
    nf = np.maximum(n, 1).astype(np.float32)
    large = MAX_EXACT + (np.log(nf / MAX_EXACT) / math.log(MAX_DISTANCE / MAX_EXACT)
                         * (NUM_BUCKETS - MAX_EXACT)).astype(np.int32)
    large = np.minimum(large, NUM_BUCKETS - 1)
    return np.where(n < MAX_EXACT, n, large).astype(np.int32)


def _band_tables():
    qi = np.arange(BLOCK)[:, None]
    ki = np.arange(2 * BLOCK)[None, :]
    dist = qi + BLOCK - ki
    in_band = ((dist >= 0) & (dist < WINDOW)).astype(np.int32)
    return _t5_causal_buckets(dist), in_band


def _group_sumsq(x, ones_blk):
    sq = x * x
    hi = sq.astype(BF16)
    lo = (sq - hi.astype(F32)).astype(BF16)
    n = x.shape[1]
    w = ones_blk.shape[0]
    outs = [_dot(hi[:, s:s + w], ones_blk) + _dot(lo[:, s:s + w], ones_blk) for s in range(0, n, w)]
    return outs[0] if len(outs) == 1 else jnp.concatenate(outs, axis=1)


def _attn_body(sink_ref, relb_ref, x_ref, g_ref, wqkv_ref, bqkv_ref, qg_ref, kg_ref, bkt_ref, band_ref,
               wo_ref, bo_ref, o_ref, bias_ref, q_ref, kext_ref, vext_ref, oacc_ref):
    ts = x_ref.shape[0]
    nblk = ts // BLOCK
    first_tile = pl.program_id(1) == 0
    kv_w = N_KV_HEADS * HEAD_DIM

    @pl.when(jnp.logical_and(pl.program_id(0) == 0, pl.program_id(1) == 0))
    def _():
        bkt = bkt_ref[...]
        band = band_ref[...] > 0
        has_prev = lax.broadcasted_iota(jnp.int32, (BLOCK, 2 * BLOCK), 1) >= BLOCK
        for kv in range(N_KV_HEADS):
            for pair in range(PAIRS_PER_KV):
                for par in range(2):
                    head = kv * GQA_GROUP + 2 * pair + par

                    def pick(b, acc):
                        return jnp.where(bkt == b, relb_ref[b, head], acc)

                    vals = lax.fori_loop(0, NUM_BUCKETS, pick, jnp.zeros((BLOCK, 2 * BLOCK), F32))
                    vals = jnp.where(band, vals, MASK_VALUE)
                    rows = slice(pair * BLOCK, (pair + 1) * BLOCK)
                    cols = slice(par * 2 * BLOCK, (par + 1) * 2 * BLOCK)
                    bias_ref[0, kv, rows, cols] = vals
                    bias_ref[1, kv, rows, cols] = jnp.where(has_prev, vals, MASK_VALUE)

    @pl.when(first_tile)
    def _():
        kext_ref[:, 0:BLOCK, :] = jnp.zeros((2 * N_KV_HEADS, BLOCK, kv_w), BF16)
        vext_ref[:, 0:BLOCK, :] = jnp.zeros((2 * N_KV_HEADS, BLOCK, kv_w), BF16)

    x = x_ref[...]
    h = _rmsnorm_bf16(x, g_ref[...])

    q = _dot(h, wqkv_ref[:, 0:D_MODEL]) + bqkv_ref[:, 0:D_MODEL]
    head_shift = HEAD_DIM.bit_length() - 1
    r4 = lax.broadcasted_iota(jnp.int32, (2 * LANES, 2 * LANES), 0) >> head_shift
    c4 = lax.broadcasted_iota(jnp.int32, (2 * LANES, 2 * LANES), 1) >> head_shift
    ones4 = jnp.where(r4 == c4, 1.0, 0.0).astype(BF16)
    qss = _group_sumsq(q, ones4)
    q_ref[...] = (q * lax.rsqrt(qss * (1.0 / HEAD_DIM) + EPS) * qg_ref[...]).astype(BF16)

    kvp = _dot(h, wqkv_ref[:, D_MODEL:D_MODEL + 2 * kv_w]) + bqkv_ref[:, D_MODEL:D_MODEL + 2 * kv_w]
    k = kvp[:, 0:kv_w]
    v = kvp[:, kv_w:2 * kv_w]
    ones2 = ones4[0:kv_w, 0:kv_w]
    kss = _group_sumsq(k, ones2)
    kn = k * lax.rsqrt(kss * (1.0 / HEAD_DIM) + EPS) * kg_ref[...]

    lane = lax.broadcasted_iota(jnp.int32, (ts, kv_w), 1)
    low = lane < HEAD_DIM
    kn_sw = pltpu.roll(kn, HEAD_DIM, 1)
    v_sw = pltpu.roll(v, HEAD_DIM, 1)
    for c in range(N_KV_HEADS):
        for p in range(2):
            half = low if p == 0 else jnp.logical_not(low)
            ksrc = kn if c == p else kn_sw
            vsrc = v if c == p else v_sw
            kext_ref[2 * c + p, BLOCK:, :] = jnp.where(half, ksrc, 0.0).astype(BF16)
            vext_ref[2 * c + p, BLOCK:, :] = jnp.where(half, vsrc, 1.0).astype(BF16)

    lane_blk = lax.broadcasted_iota(jnp.int32, (BLOCK, LANES), 1)
    low_blk = lane_blk < HEAD_DIM

    def block_step(j, carry):
        r0 = pl.multiple_of(j * BLOCK, BLOCK)
        variant = jnp.where(jnp.logical_and(first_tile, j == 0), 1, 0)
        for c in range(N_KV_HEADS):
            qs = jnp.concatenate(
                [q_ref[pl.ds(r0, BLOCK), (c * PAIRS_PER_KV + i) * LANES:(c * PAIRS_PER_KV + i + 1) * LANES]
                 for i in range(PAIRS_PER_KV)], axis=0)
            normed = []
            for p in range(2):
                kx = kext_ref[2 * c + p, pl.ds(r0, 2 * BLOCK), :]
                vx = vext_ref[2 * c + p, pl.ds(r0, 2 * BLOCK), :]
                s = lax.dot_general(qs, kx, (((1,), (1,)), ((), ())), preferred_element_type=F32)
                s = s + bias_ref[variant, c, :, p * 2 * BLOCK:(p + 1) * 2 * BLOCK]
                outs = []
                for i in range(PAIRS_PER_KV):
                    sink = sink_ref[c * GQA_GROUP + 2 * i + p]
                    si = s[i * BLOCK:(i + 1) * BLOCK]
                    m = jnp.maximum(jnp.max(si, axis=-1, keepdims=True), sink)
                    pe = jnp.exp(si - m).astype(BF16)
                    ox = _dot(pe, vx)
                    den = pltpu.roll(ox, HEAD_DIM, 1) + jnp.exp(sink - m)
                    outs.append(ox / den)
                normed.append(outs)
            for i in range(PAIRS_PER_KV):
                pair = c * PAIRS_PER_KV + i
                slab = jnp.where(low_blk, normed[0][i], normed[1][i])
                oacc_ref[pl.ds(r0, BLOCK), pair * LANES:(pair + 1) * LANES] = slab.astype(BF16)
        return carry

    lax.fori_loop(0, nblk, block_step, 0)

    kext_ref[:, 0:BLOCK, :] = kext_ref[:, ts:ts + BLOCK, :]
    vext_ref[:, 0:BLOCK, :] = vext_ref[:, ts:ts + BLOCK, :]

    o_ref[...] = x + _dot(oacc_ref[...], wo_ref[...]) + bo_ref[...]


def _attn_mixer(x, g, w_qkv, b_qkv, q_gain, k_gain, sinks, w_o, b_o, rel_bias):
    bsz, s, _ = x.shape
    ts = MIX_TOKENS
    assert s % ts == 0 and ts % BLOCK == 0
    qkv_w = (N_HEADS + 2 * N_KV_HEADS) * HEAD_DIM
    kv_w = N_KV_HEADS * HEAD_DIM
    buckets, in_band = _band_tables()
    qg = (jnp.tile(q_gain, N_HEADS) * (HEAD_DIM ** -0.5)).reshape(1, D_MODEL)
    kg = jnp.tile(k_gain, N_KV_HEADS).reshape(1, kv_w)
    smem = pl.BlockSpec(memory_space=pltpu.SMEM)
    return pl.pallas_call(
        _attn_body,
        grid=(bsz, s // ts),
        in_specs=[
            smem,
            smem,
            pl.BlockSpec((None, ts, D_MODEL), lambda b, i: (b, i, 0)),
            _const_spec((1, D_MODEL)),
            _const_spec((D_MODEL, qkv_w)),
            _const_spec((1, qkv_w)),
            _const_spec((1, D_MODEL)),
            _const_spec((1, kv_w)),
            _const_spec((BLOCK, 2 * BLOCK)),
            _const_spec((BLOCK, 2 * BLOCK)),
            _const_spec((D_MODEL, D_MODEL)),
            _const_spec((1, D_MODEL)),
        ],
        out_specs=pl.BlockSpec((None, ts, D_MODEL), lambda b, i: (b, i, 0)),
        out_shape=jax.ShapeDtypeStruct(x.shape, F32),
        scratch_shapes=[
            pltpu.VMEM((2, N_KV_HEADS, PAIRS_PER_KV * BLOCK, 4 * BLOCK), F32),
            pltpu.VMEM((ts, D_MODEL), BF16),
            pltpu.VMEM((2 * N_KV_HEADS, ts + BLOCK, kv_w), BF16),
            pltpu.VMEM((2 * N_KV_HEADS, ts + BLOCK, kv_w), BF16),
            pltpu.VMEM((ts, D_MODEL), BF16),
        ],
        compiler_params=_compiler_params(2),
        name="attn_mixer",
    )(sinks, rel_bias, x, g.reshape(1, D_MODEL), w_qkv.astype(BF16), b_qkv.reshape(1, qkv_w), qg, kg,
      jnp.asarray(buckets), jnp.asarray(in_band), w_o.astype(BF16), b_o.reshape(1, D_MODEL))


def kernel(x, ffn1_norm, ffn1_w_in, ffn1_w_out, mix_norm, ffn2_norm, ffn2_w_in, ffn2_w_out, conv_w_in, conv_w,
           conv_w_out, pool_w, pool_b, pool_scale, attn_w_qkv, attn_b_qkv, attn_q_norm, attn_k_norm, attn_sinks,
           attn_w_o, attn_b_o, rel_bias):
    bsz, s, d = x.shape

    def ffn(xx, g, w_in, w_out):
        return _ffn(xx.reshape(bsz * s, d), g, w_in, w_out).reshape(bsz, s, d)

    for i in range(DEPTH):
        x = ffn(x, ffn1_norm[i], ffn1_w_in[i], ffn1_w_out[i])
        kind, j = i % N_MIXERS, i // N_MIXERS
        if kind == 0:
            x = _conv_mixer(x, mix_norm[i], conv_w_in[j], conv_w[j], conv_w_out[j])
        elif kind == 1:
            x = _pool_mixer(x, mix_norm[i], pool_w[j], pool_b[j], pool_scale[j])
        else:
            x = _attn_mixer(x, mix_norm[i], attn_w_qkv[j], attn_b_qkv[j], attn_q_norm[j], attn_k_norm[j],
                            attn_sinks[j], attn_w_o[j], attn_b_o[j], rel_bias)
        x = ffn(x, ffn2_norm[i], ffn2_w_in[i], ffn2_w_out[i])
    return x
```

```python
import functools
import math

import numpy as np
import jax
import jax.numpy as jnp
from jax import lax
from jax.experimental import pallas as pl
from jax.experimental.pallas import tpu as pltpu

D_MODEL = 1024
D_FF = 2816
DEPTH = 4
N_MIXERS = 3
CONV_WIDTH = 3
POOL_WINDOWS = (2, 4, 8, 16)
POOL_CH = D_MODEL // len(POOL_WINDOWS)
HEAD_DIM = 64
N_HEADS = D_MODEL // HEAD_DIM
N_KV_HEADS = 2
GQA_GROUP = N_HEADS // N_KV_HEADS
WINDOW = 128
BLOCK = 128
NUM_BUCKETS = 32
MAX_EXACT = NUM_BUCKETS // 2
MAX_DISTANCE = 128
EPS = 1e-6

LANES = 128
SUBLANES = 8
MASK_VALUE = -1e30
VMEM_LIMIT_BYTES = 56 * 1024 * 1024

FFN_TOKENS = 1024
FFN_CHUNK = 256
MIX_TOKENS = 512
POOL_HALO = 16
HEAD_PAIRS = N_HEADS // 2
PAIRS_PER_KV = HEAD_PAIRS // N_KV_HEADS

F32 = jnp.float32
BF16 = jnp.bfloat16


def _rmsnorm_bf16(x, g):
    ms = jnp.mean(x * x, axis=-1, keepdims=True)
    return (x * lax.rsqrt(ms + EPS) * g).astype(BF16)


def _dot(a, b):
    return jnp.dot(a, b, preferred_element_type=F32)


def _const_spec(shape):
    return pl.BlockSpec(shape, lambda *_: (0,) * len(shape), pipeline_mode=pl.Buffered(1))


def _layer_spec(shape, layer):
    return pl.BlockSpec((None,) + shape, lambda *_: (layer,) + (0,) * len(shape), pipeline_mode=pl.Buffered(1))


def _compiler_params(n_grid_dims):
    return pltpu.CompilerParams(
        dimension_semantics=("arbitrary",) * n_grid_dims,
        vmem_limit_bytes=VMEM_LIMIT_BYTES,
    )


def _ffn_body(x_ref, g_ref, win_ref, wout_ref, o_ref):
    x = x_ref[...]
    h = _rmsnorm_bf16(x, g_ref[...])
    acc = None
    for c in range(D_FF // FFN_CHUNK):
        lo = c * FFN_CHUNK
        gate = _dot(h, win_ref[:, lo:lo + FFN_CHUNK])
        up = _dot(h, win_ref[:, D_FF + lo:D_FF + lo + FFN_CHUNK])
        act = (gate * jax.nn.sigmoid(gate) * up).astype(BF16)
        part = _dot(act, wout_ref[lo:lo + FFN_CHUNK, :])
        acc = part if acc is None else acc + part
    o_ref[...] = x + 0.5 * acc


def _ffn(x2, g, w_in, w_out, layer):
    t = x2.shape[0]
    assert t % FFN_TOKENS == 0
    return pl.pallas_call(
        _ffn_body,
        grid=(t // FFN_TOKENS,),
        in_specs=[
            pl.BlockSpec((FFN_TOKENS, D_MODEL), lambda i: (i, 0)),
            _layer_spec((1, D_MODEL), layer),
            _layer_spec((D_MODEL, 2 * D_FF), layer),
            _layer_spec((D_FF, D_MODEL), layer),
        ],
        out_specs=pl.BlockSpec((FFN_TOKENS, D_MODEL), lambda i: (i, 0)),
        out_shape=jax.ShapeDtypeStruct((t, D_MODEL), F32),
        compiler_params=_compiler_params(1),
        name="ffn",
    )(x2, g, w_in, w_out)


def _conv_body(x_ref, g_ref, win_ref, cw_ref, wout_ref, o_ref, prev_ref):
    ts = x_ref.shape[0]

    @pl.when(pl.program_id(1) == 0)
    def _():
        prev_ref[...] = jnp.zeros_like(prev_ref)

    x = x_ref[...]
    h = _rmsnorm_bf16(x, g_ref[...])
    c = _dot(h, win_ref[:, D_MODEL:2 * D_MODEL])
    v = _dot(h, win_ref[:, 2 * D_MODEL:3 * D_MODEL])
    z = c * v
    prev = prev_ref[...]
    prev_ref[...] = z[ts - SUBLANES:, :]

    w0 = cw_ref[0:1, :]
    w1 = cw_ref[1:2, :]
    w2 = cw_ref[2:3, :]
    z1 = pltpu.roll(z, 1, 0)
    z2 = pltpu.roll(z, 2, 0)
    row = lax.broadcasted_iota(jnp.int32, (SUBLANES, D_MODEL), 0)
    head1 = jnp.where(row < 1, pltpu.roll(prev, 1, 0), z1[:SUBLANES])
    head2 = jnp.where(row < 2, pltpu.roll(prev, 2, 0), z2[:SUBLANES])
    conv_head = w0 * head2 + w1 * head1 + w2 * z[:SUBLANES]
    conv_rest = w0 * z2[SUBLANES:] + w1 * z1[SUBLANES:] + w2 * z[SUBLANES:]
    conv = jnp.concatenate([conv_head, conv_rest], axis=0)

    b = _dot(h, win_ref[:, 0:D_MODEL])
    u = (b * conv).astype(BF16)
    o_ref[...] = x + _dot(u, wout_ref[...])


def _conv_mixer(x, g, w_in, conv_w, w_out, layer):
    bsz, s, _ = x.shape
    ts = MIX_TOKENS
    assert s % ts == 0
    return pl.pallas_call(
        _conv_body,
        grid=(bsz, s // ts),
        in_specs=[
            pl.BlockSpec((None, ts, D_MODEL), lambda b, i: (b, i, 0)),
            _const_spec((1, D_MODEL)),
            _layer_spec((D_MODEL, 3 * D_MODEL), layer),
            _layer_spec((CONV_WIDTH, D_MODEL), layer),
            _layer_spec((D_MODEL, D_MODEL), layer),
        ],
        out_specs=pl.BlockSpec((None, ts, D_MODEL), lambda b, i: (b, i, 0)),
        out_shape=jax.ShapeDtypeStruct(x.shape, F32),
        scratch_shapes=[pltpu.VMEM((SUBLANES, D_MODEL), F32)],
        compiler_params=_compiler_params(2),
        name="conv_mixer",
    )(x, g.reshape(1, D_MODEL), w_in, conv_w, w_out)


def _pool_body(x_ref, g_ref, w_ref, b_ref, sc_ref, o_ref, prev_ref):
    ts = x_ref.shape[0]
    i = pl.program_id(1)

    @pl.when(i == 0)
    def _():
        prev_ref[...] = jnp.zeros_like(prev_ref)

    x = x_ref[...]
    ms = jnp.mean(x * x, axis=-1, keepdims=True)
    h = x * lax.rsqrt(ms + EPS) * g_ref[...]
    prev = prev_ref[...]
    prev_ref[...] = h[ts - POOL_HALO:, :]

    pos = i * ts + lax.broadcasted_iota(jnp.int32, (ts, POOL_CH), 0)
    ys = []
    for gi, win in enumerate(POOL_WINDOWS):
        lo = gi * POOL_CH
        hg = h[:, lo:lo + POOL_CH]
        ssum = jnp.concatenate([prev[:, lo:lo + POOL_CH], hg], axis=0)
        shift = 1
        while shift < win:
            ssum = ssum + pltpu.roll(ssum, shift, 0)
            shift *= 2
        cnt = jnp.minimum(pos + 1, win).astype(F32)
        diff = ssum[POOL_HALO:] / cnt - hg
        ys.append(_dot(diff.astype(BF16), w_ref[gi]))
    y = jnp.concatenate(ys, axis=1)
    o_ref[...] = x + (y + b_ref[...]) * sc_ref[...]


def _pool_mixer(x, g, w, bias, scale):
    bsz, s, _ = x.shape
    ts = MIX_TOKENS
    assert s % ts == 0
    ng = len(POOL_WINDOWS)
    return pl.pallas_call(
        _pool_body,
        grid=(bsz, s // ts),
        in_specs=[
            pl.BlockSpec((None, ts, D_MODEL), lambda b, i: (b, i, 0)),
            _const_spec((1, D_MODEL)),
            _const_spec((ng, POOL_CH, POOL_CH)),
            _const_spec((1, D_MODEL)),
            _const_spec((1, D_MODEL)),
        ],
        out_specs=pl.BlockSpec((None, ts, D_MODEL), lambda b, i: (b, i, 0)),
        out_shape=jax.ShapeDtypeStruct(x.shape, F32),
        scratch_shapes=[pltpu.VMEM((POOL_HALO, D_MODEL), F32)],
        compiler_params=_compiler_params(2),
        name="pool_mixer",
    )(x, g.reshape(1, D_MODEL), w.astype(BF16), bias.reshape(1, D_MODEL), scale.reshape(1, D_MODEL))


def _t5_causal_buckets(n):
    nf = np.maximum(n, 1).astype(np.float32)
    large = MAX_EXACT + (np.log(nf / MAX_EXACT) / math.log(MAX_DISTANCE / MAX_EXACT)
                         * (NUM_BUCKETS - MAX_EXACT)).astype(np.int32)
    large = np.minimum(large, NUM_BUCKETS - 1)
    return np.where(n < MAX_EXACT, n, large).astype(np.int32)


def _band_tables():
    qi = np.arange(BLOCK)[:, None]
    ki = np.arange(2 * BLOCK)[None, :]
    dist = qi + BLOCK - ki
    in_band = ((dist >= 0) & (dist < WINDOW)).astype(np.int32)
    return _t5_causal_buckets(dist), in_band


def _group_sumsq(x, ones_blk):
    sq = x * x
    hi = sq.astype(BF16)
    lo = (sq - hi.astype(F32)).astype(BF16)
    n = x.shape[1]
    w = ones_blk.shape[0]
    outs = [_dot(hi[:, s:s + w], ones_blk) + _dot(lo[:, s:s + w], ones_blk) for s in range(0, n, w)]
    return outs[0] if len(outs) == 1 else jnp.concatenate(outs, axis=1)


def _attn_body(sink_ref, relb_ref, x_ref, g_ref, wqkv_ref, bqkv_ref, qg_ref, kg_ref, bkt_ref, band_ref,
               wo_ref, bo_ref, o_ref, bias_ref, q_ref, kext_ref, vext_ref, oacc_ref):
    ts = x_ref.shape[0]
    nblk = ts // BLOCK
    first_tile = pl.program_id(1) == 0
    kv_w = N_KV_HEADS * HEAD_DIM

    @pl.when(jnp.logical_and(pl.program_id(0) == 0, pl.program_id(1) == 0))
    def _():
        bkt = bkt_ref[...]
        band = band_ref[...] > 0
        has_prev = lax.broadcasted_iota(jnp.int32, (BLOCK, 2 * BLOCK), 1) >= BLOCK
        for kv in range(N_KV_HEADS):
            for pair in range(PAIRS_PER_KV):
                for par in range(2):
                    head = kv * GQA_GROUP + 2 * pair + par

                    def pick(b, acc):
                        return jnp.where(bkt == b, relb_ref[b, head], acc)

                    vals = lax.fori_loop(0, NUM_BUCKETS, pick, jnp.zeros((BLOCK, 2 * BLOCK), F32))
                    vals = jnp.where(band, vals, MASK_VALUE)
                    rows = slice(pair * BLOCK, (pair + 1) * BLOCK)
                    cols = slice(par * 2 * BLOCK, (par + 1) * 2 * BLOCK)
                    bias_ref[0, kv, rows, cols] = vals
                    bias_ref[1, kv, rows, cols] = jnp.where(has_prev, vals, MASK_VALUE)

    @pl.when(first_tile)
    def _():
        kext_ref[:, 0:BLOCK, :] = jnp.zeros((2 * N_KV_HEADS, BLOCK, kv_w), BF16)
        vext_ref[:, 0:BLOCK, :] = jnp.zeros((2 * N_KV_HEADS, BLOCK, kv_w), BF16)

    x = x_ref[...]
    h = _rmsnorm_bf16(x, g_ref[...])

    q = _dot(h, wqkv_ref[:, 0:D_MODEL]) + bqkv_ref[:, 0:D_MODEL]
    head_shift = HEAD_DIM.bit_length() - 1
    r4 = lax.broadcasted_iota(jnp.int32, (2 * LANES, 2 * LANES), 0) >> head_shift
    c4 = lax.broadcasted_iota(jnp.int32, (2 * LANES, 2 * LANES), 1) >> head_shift
    ones4 = jnp.where(r4 == c4, 1.0, 0.0).astype(BF16)
    qss = _group_sumsq(q, ones4)
    q_ref[...] = (q * lax.rsqrt(qss * (1.0 / HEAD_DIM) + EPS) * qg_ref[...]).astype(BF16)

    kvp = _dot(h, wqkv_ref[:, D_MODEL:D_MODEL + 2 * kv_w]) + bqkv_ref[:, D_MODEL:D_MODEL + 2 * kv_w]
    k = kvp[:, 0:kv_w]
    v = kvp[:, kv_w:2 * kv_w]
    ones2 = ones4[0:kv_w, 0:kv_w]
    kss = _group_sumsq(k, ones2)
    kn = k * lax.rsqrt(kss * (1.0 / HEAD_DIM) + EPS) * kg_ref[...]

    lane = lax.broadcasted_iota(jnp.int32, (ts, kv_w), 1)
    low = lane < HEAD_DIM
    kn_sw = pltpu.roll(kn, HEAD_DIM, 1)
    v_sw = pltpu.roll(v, HEAD_DIM, 1)
    for c in range(N_KV_HEADS):
        for p in range(2):
            half = low if p == 0 else jnp.logical_not(low)
            ksrc = kn if c == p else kn_sw
            vsrc = v if c == p else v_sw
            kext_ref[2 * c + p, BLOCK:, :] = jnp.where(half, ksrc, 0.0).astype(BF16)
            vext_ref[2 * c + p, BLOCK:, :] = jnp.where(half, vsrc, 1.0).astype(BF16)

    lane_blk = lax.broadcasted_iota(jnp.int32, (BLOCK, LANES), 1)
    low_blk = lane_blk < HEAD_DIM

    for j in range(nblk):
        r0 = j * BLOCK
        variant = jnp.where(first_tile, 1, 0) if j == 0 else 0
        for c in range(N_KV_HEADS):
            qs = jnp.concatenate(
                [q_ref[r0:r0 + BLOCK, (c * PAIRS_PER_KV + i) * LANES:(c * PAIRS_PER_KV + i + 1) * LANES]
                 for i in range(PAIRS_PER_KV)], axis=0)
            res = []
            for p in range(2):
                kx = kext_ref[2 * c + p, r0:r0 + 2 * BLOCK, :]
                vx = vext_ref[2 * c + p, r0:r0 + 2 * BLOCK, :]
                s = lax.dot_general(qs, kx, (((1,), (1,)), ((), ())), preferred_element_type=F32)
                s = s + bias_ref[variant, c, :, p * 2 * BLOCK:(p + 1) * 2 * BLOCK]
                outs = []
                for i in range(PAIRS_PER_KV):
                    sink = sink_ref[c * GQA_GROUP + 2 * i + p]
                    si = s[i * BLOCK:(i + 1) * BLOCK]
                    m = jnp.maximum(jnp.max(si, axis=-1, keepdims=True), sink)
                    pe = jnp.exp(si - m).astype(BF16)
                    ox = _dot(pe, vx)
                    outs.append((ox, jnp.exp(sink - m)))
                res.append(outs)
            for i in range(PAIRS_PER_KV):
                pair = c * PAIRS_PER_KV + i
                (ox_e, sk_e), (ox_o, sk_o) = res[0][i], res[1][i]
                num = jnp.where(low_blk, ox_e, ox_o)
                sums = pltpu.roll(jnp.where(low_blk, ox_o, ox_e), HEAD_DIM, 1)
                den = sums + jnp.where(low_blk, sk_e, sk_o)
                oacc_ref[r0:r0 + BLOCK, pair * LANES:(pair + 1) * LANES] = (num / den).astype(BF16)

    kext_ref[:, 0:BLOCK, :] = kext_ref[:, ts:ts + BLOCK, :]
    vext_ref[:, 0:BLOCK, :] = vext_ref[:, ts:ts + BLOCK, :]

    o_ref[...] = x + _dot(oacc_ref[...], wo_ref[...]) + bo_ref[...]


def _attn_mixer(x, g, w_qkv, b_qkv, q_gain, k_gain, sinks, w_o, b_o, rel_bias):
    bsz, s, _ = x.shape
    ts = MIX_TOKENS
    assert s % ts == 0 and ts % BLOCK == 0
    qkv_w = (N_HEADS + 2 * N_KV_HEADS) * HEAD_DIM
    kv_w = N_KV_HEADS * HEAD_DIM
    buckets, in_band = _band_tables()
    qg = (jnp.tile(q_gain, N_HEADS) * (HEAD_DIM ** -0.5)).reshape(1, D_MODEL)
    kg = jnp.tile(k_gain, N_KV_HEADS).reshape(1, kv_w)
    smem = pl.BlockSpec(memory_space=pltpu.SMEM)
    return pl.pallas_call(
        _attn_body,
        grid=(bsz, s // ts),
        in_specs=[
            smem,
            smem,
            pl.BlockSpec((None, ts, D_MODEL), lambda b, i: (b, i, 0)),
            _const_spec((1, D_MODEL)),
            _const_spec((D_MODEL, qkv_w)),
            _const_spec((1, qkv_w)),
            _const_spec((1, D_MODEL)),
            _const_spec((1, kv_w)),
            _const_spec((BLOCK, 2 * BLOCK)),
            _const_spec((BLOCK, 2 * BLOCK)),
            _const_spec((D_MODEL, D_MODEL)),
            _const_spec((1, D_MODEL)),
        ],
        out_specs=pl.BlockSpec((None, ts, D_MODEL), lambda b, i: (b, i, 0)),
        out_shape=jax.ShapeDtypeStruct(x.shape, F32),
        scratch_shapes=[
            pltpu.VMEM((2, N_KV_HEADS, PAIRS_PER_KV * BLOCK, 4 * BLOCK), F32),
            pltpu.VMEM((ts, D_MODEL), BF16),
            pltpu.VMEM((2 * N_KV_HEADS, ts + BLOCK, kv_w), BF16),
            pltpu.VMEM((2 * N_KV_HEADS, ts + BLOCK, kv_w), BF16),
            pltpu.VMEM((ts, D_MODEL), BF16),
        ],
        compiler_params=_compiler_params(2),
        name="attn_mixer",
    )(sinks, rel_bias, x, g.reshape(1, D_MODEL), w_qkv.astype(BF16), b_qkv.reshape(1, qkv_w), qg, kg,
      jnp.asarray(buckets), jnp.asarray(in_band), w_o.astype(BF16), b_o.reshape(1, D_MODEL))


def kernel(x, ffn1_norm, ffn1_w_in, ffn1_w_out, mix_norm, ffn2_norm, ffn2_w_in, ffn2_w_out, conv_w_in, conv_w,
           conv_w_out, pool_w, pool_b, pool_scale, attn_w_qkv, attn_b_qkv, attn_q_norm, attn_k_norm, attn_sinks,
           attn_w_o, attn_b_o, rel_bias):
    bsz, s, d = x.shape
    ffn1 = (ffn1_norm.reshape(DEPTH, 1, d), ffn1_w_in.astype(BF16), ffn1_w_out.astype(BF16))
    ffn2 = (ffn2_norm.reshape(DEPTH, 1, d), ffn2_w_in.astype(BF16), ffn2_w_out.astype(BF16))
    conv_w_in = conv_w_in.astype(BF16)
    conv_w_out = conv_w_out.astype(BF16)

    def ffn(xx, params, layer):
        return _ffn(xx.reshape(bsz * s, d), *params, layer).reshape(bsz, s, d)

    for i in range(DEPTH):
        x = ffn(x, ffn1, i)
        kind, j = i % N_MIXERS, i // N_MIXERS
        if kind == 0:
            x = _conv_mixer(x, mix_norm[i], conv_w_in, conv_w, conv_w_out, j)
        elif kind == 1:
            x = _pool_mixer(x, mix_norm[i], pool_w[j], pool_b[j], pool_scale[j])
        else:
            x = _attn_mixer(x, mix_norm[i], attn_w_qkv[j], attn_b_qkv[j], attn_q_norm[j], attn_k_norm[j],
                            attn_sinks[j], attn_w_o[j], attn_b_o[j], rel_bias)
        x = ffn(x, ffn2, i)
    return x
```

```python
import functools
import math

import numpy as np
import jax
import jax.numpy as jnp
from jax import lax
from jax.experimental import pallas as pl
from jax.experimental.pallas import tpu as pltpu

D_MODEL = 1024
D_FF = 2816
DEPTH = 4
N_MIXERS = 3
CONV_WIDTH = 3
POOL_WINDOWS = (2, 4, 8, 16)
POOL_CH = D_MODEL // len(POOL_WINDOWS)
HEAD_DIM = 64
N_HEADS = D_MODEL // HEAD_DIM
N_KV_HEADS = 2
GQA_GROUP = N_HEADS // N_KV_HEADS
WINDOW = 128
BLOCK = 128
NUM_BUCKETS = 32
MAX_EXACT = NUM_BUCKETS // 2
MAX_DISTANCE = 128
EPS = 1e-6

LANES = 128
SUBLANES = 8
MASK_VALUE = -1e30
VMEM_LIMIT_BYTES = 56 * 1024 * 1024

FFN_TOKENS = 1024
FFN_CHUNK = 256
MIX_TOKENS = 512
CONV_TOKENS = 1024
POOL_HALO = 16
HEAD_PAIRS = N_HEADS // 2
PAIRS_PER_KV = HEAD_PAIRS // N_KV_HEADS

F32 = jnp.float32
BF16 = jnp.bfloat16


def _rmsnorm_bf16(x, g):
    ms = jnp.mean(x * x, axis=-1, keepdims=True)
    return (x * lax.rsqrt(ms + EPS) * g).astype(BF16)


def _dot(a, b):
    return jnp.dot(a, b, preferred_element_type=F32)


def _const_spec(shape):
    return pl.BlockSpec(shape, lambda *_: (0,) * len(shape), pipeline_mode=pl.Buffered(1))


def _layer_spec(shape, layer):
    return pl.BlockSpec((None,) + shape, lambda *_: (layer,) + (0,) * len(shape), pipeline_mode=pl.Buffered(1))


def _compiler_params(n_grid_dims):
    return pltpu.CompilerParams(
        dimension_semantics=("arbitrary",) * n_grid_dims,
        vmem_limit_bytes=VMEM_LIMIT_BYTES,
    )


def _ffn_body(x_ref, g_ref, win_ref, wout_ref, o_ref):
    x = x_ref[...]
    h = _rmsnorm_bf16(x, g_ref[...])
    acc = None
    for c in range(D_FF // FFN_CHUNK):
        lo = c * FFN_CHUNK
        gate = _dot(h, win_ref[:, lo:lo + FFN_CHUNK])
        up = _dot(h, win_ref[:, D_FF + lo:D_FF + lo + FFN_CHUNK])
        act = (gate * jax.nn.sigmoid(gate) * up).astype(BF16)
        part = _dot(act, wout_ref[lo:lo + FFN_CHUNK, :])
        acc = part if acc is None else acc + part
    o_ref[...] = x + 0.5 * acc


def _ffn(x2, g, w_in, w_out, layer):
    t = x2.shape[0]
    assert t % FFN_TOKENS == 0
    return pl.pallas_call(
        _ffn_body,
        grid=(t // FFN_TOKENS,),
        in_specs=[
            pl.BlockSpec((FFN_TOKENS, D_MODEL), lambda i: (i, 0)),
            _layer_spec((1, D_MODEL), layer),
            _layer_spec((D_MODEL, 2 * D_FF), layer),
            _layer_spec((D_FF, D_MODEL), layer),
        ],
        out_specs=pl.BlockSpec((FFN_TOKENS, D_MODEL), lambda i: (i, 0)),
        out_shape=jax.ShapeDtypeStruct((t, D_MODEL), F32),
        compiler_params=_compiler_params(1),
        name="ffn",
    )(x2, g, w_in, w_out)


def _conv_body(x_ref, g_ref, win_ref, cw_ref, wout_ref, o_ref, prev_ref):
    ts = x_ref.shape[0]

    @pl.when(pl.program_id(1) == 0)
    def _():
        prev_ref[...] = jnp.zeros_like(prev_ref)

    x = x_ref[...]
    h = _rmsnorm_bf16(x, g_ref[...])
    c = _dot(h, win_ref[:, D_MODEL:2 * D_MODEL])
    v = _dot(h, win_ref[:, 2 * D_MODEL:3 * D_MODEL])
    z = c * v
    prev = prev_ref[...]
    prev_ref[...] = z[ts - SUBLANES:, :]

    w0 = cw_ref[0:1, :]
    w1 = cw_ref[1:2, :]
    w2 = cw_ref[2:3, :]
    z1 = pltpu.roll(z, 1, 0)
    z2 = pltpu.roll(z, 2, 0)
    row = lax.broadcasted_iota(jnp.int32, (SUBLANES, D_MODEL), 0)
    head1 = jnp.where(row < 1, pltpu.roll(prev, 1, 0), z1[:SUBLANES])
    head2 = jnp.where(row < 2, pltpu.roll(prev, 2, 0), z2[:SUBLANES])
    conv_head = w0 * head2 + w1 * head1 + w2 * z[:SUBLANES]
    conv_rest = w0 * z2[SUBLANES:] + w1 * z1[SUBLANES:] + w2 * z[SUBLANES:]
    conv = jnp.concatenate([conv_head, conv_rest], axis=0)

    b = _dot(h, win_ref[:, 0:D_MODEL])
    u = (b * conv).astype(BF16)
    o_ref[...] = x + _dot(u, wout_ref[...])


def _conv_mixer(x, g, w_in, conv_w, w_out, layer):
    bsz, s, _ = x.shape
    ts = CONV_TOKENS
    assert s % ts == 0
    return pl.pallas_call(
        _conv_body,
        grid=(bsz, s // ts),
        in_specs=[
            pl.BlockSpec((None, ts, D_MODEL), lambda b, i: (b, i, 0)),
            _const_spec((1, D_MODEL)),
            _layer_spec((D_MODEL, 3 * D_MODEL), layer),
            _layer_spec((CONV_WIDTH, D_MODEL), layer),
            _layer_spec((D_MODEL, D_MODEL), layer),
        ],
        out_specs=pl.BlockSpec((None, ts, D_MODEL), lambda b, i: (b, i, 0)),
        out_shape=jax.ShapeDtypeStruct(x.shape, F32),
        scratch_shapes=[pltpu.VMEM((SUBLANES, D_MODEL), F32)],
        compiler_params=_compiler_params(2),
        name="conv_mixer",
    )(x, g.reshape(1, D_MODEL), w_in, conv_w, w_out)


def _pool_body(x_ref, g_ref, w_ref, b_ref, sc_ref, o_ref, prev_ref):
    ts = x_ref.shape[0]
    i = pl.program_id(1)

    @pl.when(i == 0)
    def _():
        prev_ref[...] = jnp.zeros_like(prev_ref)

    x = x_ref[...]
    ms = jnp.mean(x * x, axis=-1, keepdims=True)
    h = x * lax.rsqrt(ms + EPS) * g_ref[...]
    prev = prev_ref[...]
    prev_ref[...] = h[ts - POOL_HALO:, :]

    pos = i * ts + lax.broadcasted_iota(jnp.int32, (ts, POOL_CH), 0)
    ys = []
    for gi, win in enumerate(POOL_WINDOWS):
        lo = gi * POOL_CH
        hg = h[:, lo:lo + POOL_CH]
        ssum = jnp.concatenate([prev[:, lo:lo + POOL_CH], hg], axis=0)
        shift = 1
        while shift < win:
            ssum = ssum + pltpu.roll(ssum, shift, 0)
            shift *= 2
        cnt = jnp.minimum(pos + 1, win).astype(F32)
        diff = ssum[POOL_HALO:] / cnt - hg
        ys.append(_dot(diff.astype(BF16), w_ref[gi]))
    y = jnp.concatenate(ys, axis=1)
    o_ref[...] = x + (y + b_ref[...]) * sc_ref[...]


def _pool_mixer(x, g, w, bias, scale):
    bsz, s, _ = x.shape
    ts = CONV_TOKENS
    assert s % ts == 0
    ng = len(POOL_WINDOWS)
    return pl.pallas_call(
        _pool_body,
        grid=(bsz, s // ts),
        in_specs=[
            pl.BlockSpec((None, ts, D_MODEL), lambda b, i: (b, i, 0)),
            _const_spec((1, D_MODEL)),
            _const_spec((ng, POOL_CH, POOL_CH)),
            _const_spec((1, D_MODEL)),
            _const_spec((1, D_MODEL)),
        ],
        out_specs=pl.BlockSpec((None, ts, D_MODEL), lambda b, i: (b, i, 0)),
        out_shape=jax.ShapeDtypeStruct(x.shape, F32),
        scratch_shapes=[pltpu.VMEM((POOL_HALO, D_MODEL), F32)],
        compiler_params=_compiler_params(2),
        name="pool_mixer",
    )(x, g.reshape(1, D_MODEL), w.astype(BF16), bias.reshape(1, D_MODEL), scale.reshape(1, D_MODEL))


def _t5_causal_buckets(n):
    nf = np.maximum(n, 1).astype(np.float32)
    large = MAX_EXACT + (np.log(nf / MAX_EXACT) / math.log(MAX_DISTANCE / MAX_EXACT)
                         * (NUM_BUCKETS - MAX_EXACT)).astype(np.int32)
    large = np.minimum(large, NUM_BUCKETS - 1)
    return np.where(n < MAX_EXACT, n, large).astype(np.int32)


def _band_tables():
    qi = np.arange(BLOCK)[:, None]
    ki = np.arange(2 * BLOCK)[None, :]
    dist = qi + BLOCK - ki
    in_band = ((dist >= 0) & (dist < WINDOW)).astype(np.int32)
    assert not in_band[:, 0].any()
    return _t5_causal_buckets(dist), in_band


def _group_sumsq(x, ones_blk):
    sq = x * x
    hi = sq.astype(BF16)
    lo = (sq - hi.astype(F32)).astype(BF16)
    n = x.shape[1]
    w = ones_blk.shape[0]
    outs = [_dot(hi[:, s:s + w], ones_blk) + _dot(lo[:, s:s + w], ones_blk) for s in range(0, n, w)]
    return outs[0] if len(outs) == 1 else jnp.concatenate(outs, axis=1)


def _attn_body(sink_ref, relb_ref, x_ref, g_ref, wqkv_ref, bqkv_ref, qg_ref, kg_ref, bkt_ref, band_ref,
               wo_ref, bo_ref, o_ref, bias_ref, q_ref, kext_ref, vext_ref, oacc_ref):
    ts = x_ref.shape[0]
    nblk = ts // BLOCK
    first_tile = pl.program_id(1) == 0
    kv_w = N_KV_HEADS * HEAD_DIM

    @pl.when(jnp.logical_and(pl.program_id(0) == 0, pl.program_id(1) == 0))
    def _():
        bkt = bkt_ref[...]
        band = band_ref[...] > 0
        key = lax.broadcasted_iota(jnp.int32, (BLOCK, 2 * BLOCK), 1)
        has_prev = key >= BLOCK
        sink_slot = key == 0
        for kv in range(N_KV_HEADS):
            for pair in range(PAIRS_PER_KV):
                for par in range(2):
                    head = kv * GQA_GROUP + 2 * pair + par

                    def pick(b, acc):
                        return jnp.where(bkt == b, relb_ref[b, head], acc)

                    vals = lax.fori_loop(0, NUM_BUCKETS, pick, jnp.zeros((BLOCK, 2 * BLOCK), F32))
                    vals = jnp.where(band, vals, MASK_VALUE)
                    sink = sink_ref[head]
                    rows = slice(pair * BLOCK, (pair + 1) * BLOCK)
                    cols = slice(par * 2 * BLOCK, (par + 1) * 2 * BLOCK)
                    bias_ref[0, kv, rows, cols] = jnp.where(sink_slot, sink, vals)
                    bias_ref[1, kv, rows, cols] = jnp.where(sink_slot, sink, jnp.where(has_prev, vals, MASK_VALUE))

    @pl.when(first_tile)
    def _():
        kext_ref[:, 0:BLOCK, :] = jnp.zeros((2 * N_KV_HEADS, BLOCK, kv_w), BF16)
        low_prev = lax.broadcasted_iota(jnp.int32, (BLOCK, kv_w), 1) < HEAD_DIM
        for c in range(N_KV_HEADS):
            vext_ref[2 * c, 0:BLOCK, :] = jnp.where(low_prev, 0.0, 1.0).astype(BF16)
            vext_ref[2 * c + 1, 0:BLOCK, :] = jnp.where(low_prev, 1.0, 0.0).astype(BF16)

    x = x_ref[...]
    h = _rmsnorm_bf16(x, g_ref[...])

    q = _dot(h, wqkv_ref[:, 0:D_MODEL]) + bqkv_ref[:, 0:D_MODEL]
    head_shift = HEAD_DIM.bit_length() - 1
    r4 = lax.broadcasted_iota(jnp.int32, (2 * LANES, 2 * LANES), 0) >> head_shift
    c4 = lax.broadcasted_iota(jnp.int32, (2 * LANES, 2 * LANES), 1) >> head_shift
    ones4 = jnp.where(r4 == c4, 1.0, 0.0).astype(BF16)
    qss = _group_sumsq(q, ones4)
    q_ref[...] = (q * lax.rsqrt(qss + HEAD_DIM * EPS) * qg_ref[...]).astype(BF16)

    kvp = _dot(h, wqkv_ref[:, D_MODEL:D_MODEL + 2 * kv_w]) + bqkv_ref[:, D_MODEL:D_MODEL + 2 * kv_w]
    k = kvp[:, 0:kv_w]
    v = kvp[:, kv_w:2 * kv_w]
    ones2 = ones4[0:kv_w, 0:kv_w]
    kss = _group_sumsq(k, ones2)
    kn = k * lax.rsqrt(kss * (1.0 / HEAD_DIM) + EPS) * kg_ref[...]

    lane = lax.broadcasted_iota(jnp.int32, (ts, kv_w), 1)
    low = lane < HEAD_DIM
    kn_sw = pltpu.roll(kn, HEAD_DIM, 1)
    v_sw = pltpu.roll(v, HEAD_DIM, 1)
    for c in range(N_KV_HEADS):
        for p in range(2):
            half = low if p == 0 else jnp.logical_not(low)
            ksrc = kn if c == p else kn_sw
            vsrc = v if c == p else v_sw
            kext_ref[2 * c + p, BLOCK:, :] = jnp.where(half, ksrc, 0.0).astype(BF16)
            vext_ref[2 * c + p, BLOCK:, :] = jnp.where(half, vsrc, 1.0).astype(BF16)

    low_blk = lax.broadcasted_iota(jnp.int32, (BLOCK, LANES), 1) < HEAD_DIM
    bf16_rows = 2 * SUBLANES
    slot0 = lax.broadcasted_iota(jnp.int32, (bf16_rows, LANES), 0) == 0
    low0 = lax.broadcasted_iota(jnp.int32, (bf16_rows, LANES), 1) < HEAD_DIM
    zeros0 = jnp.zeros((bf16_rows, LANES), BF16)

    def clear_slot0(a, mask):
        return jnp.concatenate([jnp.where(mask, zeros0, a[:bf16_rows]), a[bf16_rows:]], axis=0)

    for j in range(nblk):
        r0 = j * BLOCK
        variant = jnp.where(first_tile, 1, 0) if j == 0 else 0
        for c in range(N_KV_HEADS):
            qs = jnp.concatenate(
                [q_ref[r0:r0 + BLOCK, (c * PAIRS_PER_KV + i) * LANES:(c * PAIRS_PER_KV + i + 1) * LANES]
                 for i in range(PAIRS_PER_KV)], axis=0)
            ox = []
            for p in range(2):
                v_half = low0 if p == 0 else jnp.logical_not(low0)
                kx = clear_slot0(kext_ref[2 * c + p, r0:r0 + 2 * BLOCK, :], slot0)
                vx = clear_slot0(vext_ref[2 * c + p, r0:r0 + 2 * BLOCK, :], jnp.logical_and(slot0, v_half))
                s = lax.dot_general(qs, kx, (((1,), (1,)), ((), ())), preferred_element_type=F32)
                s = s + bias_ref[variant, c, :, p * 2 * BLOCK:(p + 1) * 2 * BLOCK]
                pes = []
                for i in range(PAIRS_PER_KV):
                    si = s[i * BLOCK:(i + 1) * BLOCK]
                    m = jnp.max(si, axis=-1, keepdims=True)
                    pes.append(jnp.exp(si - m).astype(BF16))
                ox.append(_dot(jnp.concatenate(pes, axis=0), vx))
            for i in range(PAIRS_PER_KV):
                pair = c * PAIRS_PER_KV + i
                ox_e = ox[0][i * BLOCK:(i + 1) * BLOCK]
                ox_o = ox[1][i * BLOCK:(i + 1) * BLOCK]
                num = jnp.where(low_blk, ox_e, ox_o)
                den = pltpu.roll(jnp.where(low_blk, ox_o, ox_e), HEAD_DIM, 1)
                oacc_ref[r0:r0 + BLOCK, pair * LANES:(pair + 1) * LANES] = (num / den).astype(BF16)

    kext_ref[:, 0:BLOCK, :] = kext_ref[:, ts:ts + BLOCK, :]
    vext_ref[:, 0:BLOCK, :] = vext_ref[:, ts:ts + BLOCK, :]

    o_ref[...] = x + _dot(oacc_ref[...], wo_ref[...]) + bo_ref[...]


def _attn_mixer(x, g, w_qkv, b_qkv, q_gain, k_gain, sinks, w_o, b_o, rel_bias):
    bsz, s, _ = x.shape
    ts = MIX_TOKENS
    assert s % ts == 0 and ts % BLOCK == 0
    qkv_w = (N_HEADS + 2 * N_KV_HEADS) * HEAD_DIM
    kv_w = N_KV_HEADS * HEAD_DIM
    buckets, in_band = _band_tables()
    qg = jnp.tile(q_gain, N_HEADS).reshape(1, D_MODEL)
    kg = jnp.tile(k_gain, N_KV_HEADS).reshape(1, kv_w)
    smem = pl.BlockSpec(memory_space=pltpu.SMEM)
    return pl.pallas_call(
        _attn_body,
        grid=(bsz, s // ts),
        in_specs=[
            smem,
            smem,
            pl.BlockSpec((None, ts, D_MODEL), lambda b, i: (b, i, 0)),
            _const_spec((1, D_MODEL)),
            _const_spec((D_MODEL, qkv_w)),
            _const_spec((1, qkv_w)),
            _const_spec((1, D_MODEL)),
            _const_spec((1, kv_w)),
            _const_spec((BLOCK, 2 * BLOCK)),
            _const_spec((BLOCK, 2 * BLOCK)),
            _const_spec((D_MODEL, D_MODEL)),
            _const_spec((1, D_MODEL)),
        ],
        out_specs=pl.BlockSpec((None, ts, D_MODEL), lambda b, i: (b, i, 0)),
        out_shape=jax.ShapeDtypeStruct(x.shape, F32),
        scratch_shapes=[
            pltpu.VMEM((2, N_KV_HEADS, PAIRS_PER_KV * BLOCK, 4 * BLOCK), F32),
            pltpu.VMEM((ts, D_MODEL), BF16),
            pltpu.VMEM((2 * N_KV_HEADS, ts + BLOCK, kv_w), BF16),
            pltpu.VMEM((2 * N_KV_HEADS, ts + BLOCK, kv_w), BF16),
            pltpu.VMEM((ts, D_MODEL), BF16),
        ],
        compiler_params=_compiler_params(2),
        name="attn_mixer",
    )(sinks, rel_bias, x, g.reshape(1, D_MODEL), w_qkv.astype(BF16), b_qkv.reshape(1, qkv_w), qg, kg,
      jnp.asarray(buckets), jnp.asarray(in_band), w_o.astype(BF16), b_o.reshape(1, D_MODEL))


def kernel(x, ffn1_norm, ffn1_w_in, ffn1_w_out, mix_norm, ffn2_norm, ffn2_w_in, ffn2_w_out, conv_w_in, conv_w,
           conv_w_out, pool_w, pool_b, pool_scale, attn_w_qkv, attn_b_qkv, attn_q_norm, attn_k_norm, attn_sinks,
           attn_w_o, attn_b_o, rel_bias):
    bsz, s, d = x.shape
    ffn1 = (ffn1_norm.reshape(DEPTH, 1, d), ffn1_w_in.astype(BF16), ffn1_w_out.astype(BF16))
    ffn2 = (ffn2_norm.reshape(DEPTH, 1, d), ffn2_w_in.astype(BF16), ffn2_w_out.astype(BF16))
    conv_w_in = conv_w_in.astype(BF16)
    conv_w_out = conv_w_out.astype(BF16)

    def ffn(xx, params, layer):
        return _ffn(xx.reshape(bsz * s, d), *params, layer).reshape(bsz, s, d)

    for i in range(DEPTH):
        x = ffn(x, ffn1, i)
        kind, j = i % N_MIXERS, i // N_MIXERS
        if kind == 0:
            x = _conv_mixer(x, mix_norm[i], conv_w_in, conv_w, conv_w_out, j)
        elif kind == 1:
            x = _pool_mixer(x, mix_norm[i], pool_w[j], pool_b[j], pool_scale[j])
        else:
            x = _attn_mixer(x, mix_norm[i], attn_w_qkv[j], attn_b_qkv[j], attn_q_norm[j], attn_k_norm[j],
                            attn_sinks[j], attn_w_o[j], attn_b_o[j], rel_bias)
        x = ffn(x, ffn2, i)
    return x
```

```python
import functools
import math

import numpy as np
import jax
import jax.numpy as jnp
from jax import lax
from jax.experimental import pallas as pl
from jax.experimental.pallas import tpu as pltpu

D_MODEL = 1024
D_FF = 2816
DEPTH = 4
N_MIXERS = 3
CONV_WIDTH = 3
POOL_WINDOWS = (2, 4, 8, 16)
POOL_CH = D_MODEL // len(POOL_WINDOWS)
HEAD_DIM = 64
N_HEADS = D_MODEL // HEAD_DIM
N_KV_HEADS = 2
GQA_GROUP = N_HEADS // N_KV_HEADS
WINDOW = 128
BLOCK = 128
NUM_BUCKETS = 32
MAX_EXACT = NUM_BUCKETS // 2
MAX_DISTANCE = 128
EPS = 1e-6

LANES = 128
SUBLANES = 8
MASK_VALUE = -1e30
VMEM_LIMIT_BYTES = 56 * 1024 * 1024

FFN_TOKENS = 1024
FFN_CHUNK = 256
FFN_HEAD_ROWS = 256
FFN_WIN_ROWS = 128
FFN_WOUT_ROWS = 256
MIX_TOKENS = 1024
CONV_TOKENS = 1024
CONV_SUB = 512
CONV_W_ROWS = 256
POOL_HALO = 16
HEAD_PAIRS = N_HEADS // 2
PAIRS_PER_KV = HEAD_PAIRS // N_KV_HEADS

F32 = jnp.float32
BF16 = jnp.bfloat16


def _rmsnorm_bf16(x, g):
    ms = jnp.mean(x * x, axis=-1, keepdims=True)
    return (x * lax.rsqrt(ms + EPS) * g).astype(BF16)


def _dot(a, b):
    return jnp.dot(a, b, preferred_element_type=F32)


def _const_spec(shape):
    return pl.BlockSpec(shape, lambda *_: (0,) * len(shape), pipeline_mode=pl.Buffered(1))


def _layer_spec(shape, layer):
    return pl.BlockSpec((None,) + shape, lambda *_: (layer,) + (0,) * len(shape), pipeline_mode=pl.Buffered(1))


def _fetch_weight_bf16(src, dst_ref, stage_ref, sem, rows):
    n = src.shape[0] // rows
    assert n * rows == src.shape[0]

    def copy(k):
        return pltpu.make_async_copy(src.at[pl.ds(k * rows, rows)], stage_ref.at[k % 2], sem.at[k % 2])

    copy(0).start()
    for k in range(n):
        if k + 1 < n:
            copy(k + 1).start()
        copy(k).wait()
        dst_ref[k * rows:(k + 1) * rows, :] = stage_ref[k % 2].astype(BF16)


def _compiler_params(n_grid_dims):
    return pltpu.CompilerParams(
        dimension_semantics=("arbitrary",) * n_grid_dims,
        vmem_limit_bytes=VMEM_LIMIT_BYTES,
    )


def _ffn_body(x_ref, g_ref, win_hbm, wout_hbm, o_ref, win_ref, wout_ref, stage_in, stage_out, sem_in, sem_out, *,
              layer):
    @pl.when(pl.program_id(0) == 0)
    def _():
        _fetch_weight_bf16(win_hbm.at[layer], win_ref, stage_in, sem_in, FFN_WIN_ROWS)
        _fetch_weight_bf16(wout_hbm.at[layer], wout_ref, stage_out, sem_out, FFN_WOUT_ROWS)

    x = x_ref[...]
    h = _rmsnorm_bf16(x, g_ref[...])

    def proj(c, col0):
        w = win_ref[:, col0 + c * FFN_CHUNK:col0 + (c + 1) * FFN_CHUNK]
        if c > 0:
            return _dot(h, w)
        return jnp.concatenate([_dot(h[r:r + FFN_HEAD_ROWS], w) for r in range(0, FFN_TOKENS, FFN_HEAD_ROWS)], axis=0)

    acc = None
    for c in range(D_FF // FFN_CHUNK):
        lo = c * FFN_CHUNK
        gate = proj(c, 0)
        up = proj(c, D_FF)
        act = (gate * jax.nn.sigmoid(gate) * up).astype(BF16)
        part = _dot(act, wout_ref[lo:lo + FFN_CHUNK, :])
        acc = part if acc is None else acc + part
    o_ref[...] = x + 0.5 * acc


def _ffn(x2, g, w_in, w_out, layer):
    t = x2.shape[0]
    assert t % FFN_TOKENS == 0
    return pl.pallas_call(
        functools.partial(_ffn_body, layer=layer),
        grid=(t // FFN_TOKENS,),
        in_specs=[
            pl.BlockSpec((FFN_TOKENS, D_MODEL), lambda i: (i, 0)),
            _layer_spec((1, D_MODEL), layer),
            pl.BlockSpec(memory_space=pl.ANY),
            pl.BlockSpec(memory_space=pl.ANY),
        ],
        out_specs=pl.BlockSpec((FFN_TOKENS, D_MODEL), lambda i: (i, 0)),
        out_shape=jax.ShapeDtypeStruct((t, D_MODEL), F32),
        scratch_shapes=[
            pltpu.VMEM((D_MODEL, 2 * D_FF), BF16),
            pltpu.VMEM((D_FF, D_MODEL), BF16),
            pltpu.VMEM((2, FFN_WIN_ROWS, 2 * D_FF), F32),
            pltpu.VMEM((2, FFN_WOUT_ROWS, D_MODEL), F32),
            pltpu.SemaphoreType.DMA((2,)),
            pltpu.SemaphoreType.DMA((2,)),
        ],
        compiler_params=_compiler_params(1),
        name="ffn",
    )(x2, g, w_in, w_out)


def _conv_body(x_ref, g_ref, win_hbm, cw_ref, wout_hbm, o_ref, prev_ref, win_ref, wout_ref, stage_in, stage_out,
               sem_in, sem_out, *, layer):
    ts = x_ref.shape[0]

    @pl.when(jnp.logical_and(pl.program_id(0) == 0, pl.program_id(1) == 0))
    def _():
        _fetch_weight_bf16(win_hbm.at[layer], win_ref, stage_in, sem_in, CONV_W_ROWS)
        _fetch_weight_bf16(wout_hbm.at[layer], wout_ref, stage_out, sem_out, CONV_W_ROWS)

    @pl.when(pl.program_id(1) == 0)
    def _():
        prev_ref[...] = jnp.zeros_like(prev_ref)

    w0 = cw_ref[0:1, :]
    w1 = cw_ref[1:2, :]
    w2 = cw_ref[2:3, :]
    row = lax.broadcasted_iota(jnp.int32, (SUBLANES, D_MODEL), 0)
    prev = prev_ref[...]
    for r in range(0, ts, CONV_SUB):
        x = x_ref[r:r + CONV_SUB, :]
        h = _rmsnorm_bf16(x, g_ref[...])
        c = _dot(h, win_ref[:, D_MODEL:2 * D_MODEL])
        v = _dot(h, win_ref[:, 2 * D_MODEL:3 * D_MODEL])
        z = c * v
        z1 = pltpu.roll(z, 1, 0)
        z2 = pltpu.roll(z, 2, 0)
        head1 = jnp.where(row < 1, pltpu.roll(prev, 1, 0), z1[:SUBLANES])
        head2 = jnp.where(row < 2, pltpu.roll(prev, 2, 0), z2[:SUBLANES])
        conv_head = w0 * head2 + w1 * head1 + w2 * z[:SUBLANES]
        conv_rest = w0 * z2[SUBLANES:] + w1 * z1[SUBLANES:] + w2 * z[SUBLANES:]
        conv = jnp.concatenate([conv_head, conv_rest], axis=0)
        prev = z[CONV_SUB - SUBLANES:, :]

        b = _dot(h, win_ref[:, 0:D_MODEL])
        u = (b * conv).astype(BF16)
        o_ref[r:r + CONV_SUB, :] = x + _dot(u, wout_ref[...])
    prev_ref[...] = prev


def _conv_mixer(x, g, w_in, conv_w, w_out, layer):
    bsz, s, _ = x.shape
    ts = CONV_TOKENS
    assert s % ts == 0
    return pl.pallas_call(
        functools.partial(_conv_body, layer=layer),
        grid=(bsz, s // ts),
        in_specs=[
            pl.BlockSpec((None, ts, D_MODEL), lambda b, i: (b, i, 0)),
            _const_spec((1, D_MODEL)),
            pl.BlockSpec(memory_space=pl.ANY),
            _layer_spec((CONV_WIDTH, D_MODEL), layer),
            pl.BlockSpec(memory_space=pl.ANY),
        ],
        out_specs=pl.BlockSpec((None, ts, D_MODEL), lambda b, i: (b, i, 0)),
        out_shape=jax.ShapeDtypeStruct(x.shape, F32),
        scratch_shapes=[
            pltpu.VMEM((SUBLANES, D_MODEL), F32),
            pltpu.VMEM((D_MODEL, 3 * D_MODEL), BF16),
            pltpu.VMEM((D_MODEL, D_MODEL), BF16),
            pltpu.VMEM((2, CONV_W_ROWS, 3 * D_MODEL), F32),
            pltpu.VMEM((2, CONV_W_ROWS, D_MODEL), F32),
            pltpu.SemaphoreType.DMA((2,)),
            pltpu.SemaphoreType.DMA((2,)),
        ],
        compiler_params=_compiler_params(2),
        name="conv_mixer",
    )(x, g.reshape(1, D_MODEL), w_in, conv_w, w_out)


def _pool_body(x_ref, g_ref, w_ref, b_ref, sc_ref, o_ref, prev_ref):
    ts = x_ref.shape[0]
    i = pl.program_id(1)

    @pl.when(i == 0)
    def _():
        prev_ref[...] = jnp.zeros_like(prev_ref)

    x = x_ref[...]
    ms = jnp.mean(x * x, axis=-1, keepdims=True)
    h = x * lax.rsqrt(ms + EPS) * g_ref[...]
    prev = prev_ref[...]
    prev_ref[...] = h[ts - POOL_HALO:, :]

    pos = i * ts + lax.broadcasted_iota(jnp.int32, (ts, POOL_CH), 0)
    ys = []
    for gi, win in enumerate(POOL_WINDOWS):
        lo = gi * POOL_CH
        hg = h[:, lo:lo + POOL_CH]
        ssum = jnp.concatenate([prev[:, lo:lo + POOL_CH], hg], axis=0)
        shift = 1
        while shift < win:
            ssum = ssum + pltpu.roll(ssum, shift, 0)
            shift *= 2
        cnt = jnp.minimum(pos + 1, win).astype(F32)
        diff = ssum[POOL_HALO:] / cnt - hg
        ys.append(_dot(diff.astype(BF16), w_ref[gi]))
    y = jnp.concatenate(ys, axis=1)
    o_ref[...] = x + (y + b_ref[...]) * sc_ref[...]


def _pool_mixer(x, g, w, bias, scale):
    bsz, s, _ = x.shape
    ts = CONV_TOKENS
    assert s % ts == 0
    ng = len(POOL_WINDOWS)
    return pl.pallas_call(
        _pool_body,
        grid=(bsz, s // ts),
        in_specs=[
            pl.BlockSpec((None, ts, D_MODEL), lambda b, i: (b, i, 0)),
            _const_spec((1, D_MODEL)),
            _const_spec((ng, POOL_CH, POOL_CH)),
            _const_spec((1, D_MODEL)),
            _const_spec((1, D_MODEL)),
        ],
        out_specs=pl.BlockSpec((None, ts, D_MODEL), lambda b, i: (b, i, 0)),
        out_shape=jax.ShapeDtypeStruct(x.shape, F32),
        scratch_shapes=[pltpu.VMEM((POOL_HALO, D_MODEL), F32)],
        compiler_params=_compiler_params(2),
        name="pool_mixer",
    )(x, g.reshape(1, D_MODEL), w.astype(BF16), bias.reshape(1, D_MODEL), scale.reshape(1, D_MODEL))


def _t5_causal_buckets(n):
    nf = np.maximum(n, 1).astype(np.float32)
    large = MAX_EXACT + (np.log(nf / MAX_EXACT) / math.log(MAX_DISTANCE / MAX_EXACT)
                         * (NUM_BUCKETS - MAX_EXACT)).astype(np.int32)
    large = np.minimum(large, NUM_BUCKETS - 1)
    return np.where(n < MAX_EXACT, n, large).astype(np.int32)


def _band_tables():
    qi = np.arange(BLOCK)[:, None]
    ki = np.arange(2 * BLOCK)[None, :]
    dist = qi + BLOCK - ki
    in_band = ((dist >= 0) & (dist < WINDOW)).astype(np.int32)
    assert not in_band[:, 0].any()
    return _t5_causal_buckets(dist), in_band


def _group_sumsq(x, ones_blk):
    sq = x * x
    hi = sq.astype(BF16)
    lo = (sq - hi.astype(F32)).astype(BF16)
    n = x.shape[1]
    w = ones_blk.shape[0]
    outs = [_dot(hi[:, s:s + w], ones_blk) + _dot(lo[:, s:s + w], ones_blk) for s in range(0, n, w)]
    return outs[0] if len(outs) == 1 else jnp.concatenate(outs, axis=1)


def _attn_body(sink_ref, relb_ref, x_ref, g_ref, wqkv_ref, bqkv_ref, qg_ref, kg_ref, bkt_ref, band_ref,
               wo_ref, bo_ref, o_ref, bias_ref, q_ref, kext_ref, vext_ref, oacc_ref):
    ts = x_ref.shape[0]
    nblk = ts // BLOCK
    first_tile = pl.program_id(1) == 0
    kv_w = N_KV_HEADS * HEAD_DIM

    @pl.when(jnp.logical_and(pl.program_id(0) == 0, pl.program_id(1) == 0))
    def _():
        bkt = bkt_ref[...]
        band = band_ref[...] > 0
        key = lax.broadcasted_iota(jnp.int32, (BLOCK, 2 * BLOCK), 1)
        has_prev = key >= BLOCK
        sink_slot = key == 0
        for kv in range(N_KV_HEADS):
            for pair in range(PAIRS_PER_KV):
                for par in range(2):
                    head = kv * GQA_GROUP + 2 * pair + par

                    def pick(b, acc):
                        return jnp.where(bkt == b, relb_ref[b, head], acc)

                    vals = lax.fori_loop(0, NUM_BUCKETS, pick, jnp.zeros((BLOCK, 2 * BLOCK), F32))
                    vals = jnp.where(band, vals, MASK_VALUE)
                    sink = sink_ref[head]
                    rows = slice(pair * BLOCK, (pair + 1) * BLOCK)
                    cols = slice(par * 2 * BLOCK, (par + 1) * 2 * BLOCK)
                    bias_ref[0, kv, rows, cols] = jnp.where(sink_slot, sink, vals)
                    bias_ref[1, kv, rows, cols] = jnp.where(sink_slot, sink, jnp.where(has_prev, vals, MASK_VALUE))

    @pl.when(first_tile)
    def _():
        kext_ref[:, 0:BLOCK, :] = jnp.zeros((2 * N_KV_HEADS, BLOCK, kv_w), BF16)
        low_prev = lax.broadcasted_iota(jnp.int32, (BLOCK, kv_w), 1) < HEAD_DIM
        for c in range(N_KV_HEADS):
            vext_ref[2 * c, 0:BLOCK, :] = jnp.where(low_prev, 0.0, 1.0).astype(BF16)
            vext_ref[2 * c + 1, 0:BLOCK, :] = jnp.where(low_prev, 1.0, 0.0).astype(BF16)

    x = x_ref[...]
    h = _rmsnorm_bf16(x, g_ref[...])

    q = _dot(h, wqkv_ref[:, 0:D_MODEL]) + bqkv_ref[:, 0:D_MODEL]
    head_shift = HEAD_DIM.bit_length() - 1
    r4 = lax.broadcasted_iota(jnp.int32, (2 * LANES, 2 * LANES), 0) >> head_shift
    c4 = lax.broadcasted_iota(jnp.int32, (2 * LANES, 2 * LANES), 1) >> head_shift
    ones4 = jnp.where(r4 == c4, 1.0, 0.0).astype(BF16)
    qss = _group_sumsq(q, ones4)
    q_ref[...] = (q * lax.rsqrt(qss + HEAD_DIM * EPS) * qg_ref[...]).astype(BF16)

    kvp = _dot(h, wqkv_ref[:, D_MODEL:D_MODEL + 2 * kv_w]) + bqkv_ref[:, D_MODEL:D_MODEL + 2 * kv_w]
    k = kvp[:, 0:kv_w]
    v = kvp[:, kv_w:2 * kv_w]
    ones2 = ones4[0:kv_w, 0:kv_w]
    kss = _group_sumsq(k, ones2)
    kn = k * lax.rsqrt(kss * (1.0 / HEAD_DIM) + EPS) * kg_ref[...]

    lane = lax.broadcasted_iota(jnp.int32, (ts, kv_w), 1)
    low = lane < HEAD_DIM
    kn_sw = pltpu.roll(kn, HEAD_DIM, 1)
    v_sw = pltpu.roll(v, HEAD_DIM, 1)
    for c in range(N_KV_HEADS):
        for p in range(2):
            half = low if p == 0 else jnp.logical_not(low)
            ksrc = kn if c == p else kn_sw
            vsrc = v if c == p else v_sw
            kext_ref[2 * c + p, BLOCK:, :] = jnp.where(half, ksrc, 0.0).astype(BF16)
            vext_ref[2 * c + p, BLOCK:, :] = jnp.where(half, vsrc, 1.0).astype(BF16)

    low_blk = lax.broadcasted_iota(jnp.int32, (BLOCK, LANES), 1) < HEAD_DIM
    bf16_rows = 2 * SUBLANES
    slot0 = lax.broadcasted_iota(jnp.int32, (bf16_rows, LANES), 0) == 0
    low0 = lax.broadcasted_iota(jnp.int32, (bf16_rows, LANES), 1) < HEAD_DIM
    zeros0 = jnp.zeros((bf16_rows, LANES), BF16)

    def clear_slot0(a, mask):
        return jnp.concatenate([jnp.where(mask, zeros0, a[:bf16_rows]), a[bf16_rows:]], axis=0)

    for j in range(nblk):
        r0 = j * BLOCK
        variant = jnp.where(first_tile, 1, 0) if j == 0 else 0
        for c in range(N_KV_HEADS):
            qs = jnp.concatenate(
                [q_ref[r0:r0 + BLOCK, (c * PAIRS_PER_KV + i) * LANES:(c * PAIRS_PER_KV + i + 1) * LANES]
                 for i in range(PAIRS_PER_KV)], axis=0)
            ox = []
            for p in range(2):
                v_half = low0 if p == 0 else jnp.logical_not(low0)
                kx = clear_slot0(kext_ref[2 * c + p, r0:r0 + 2 * BLOCK, :], slot0)
                vx = clear_slot0(vext_ref[2 * c + p, r0:r0 + 2 * BLOCK, :], jnp.logical_and(slot0, v_half))
                s = lax.dot_general(qs, kx, (((1,), (1,)), ((), ())), preferred_element_type=F32)
                s = s + bias_ref[variant, c, :, p * 2 * BLOCK:(p + 1) * 2 * BLOCK]
                pes = []
                for i in range(PAIRS_PER_KV):
                    si = s[i * BLOCK:(i + 1) * BLOCK]
                    m = jnp.max(si, axis=-1, keepdims=True)
                    pes.append(jnp.exp(si - m).astype(BF16))
                ox.append(_dot(jnp.concatenate(pes, axis=0), vx))
            for i in range(PAIRS_PER_KV):
                pair = c * PAIRS_PER_KV + i
                ox_e = ox[0][i * BLOCK:(i + 1) * BLOCK]
                ox_o = ox[1][i * BLOCK:(i + 1) * BLOCK]
                num = jnp.where(low_blk, ox_e, ox_o)
                den = pltpu.roll(jnp.where(low_blk, ox_o, ox_e), HEAD_DIM, 1)
                oacc_ref[r0:r0 + BLOCK, pair * LANES:(pair + 1) * LANES] = (num / den).astype(BF16)

    kext_ref[:, 0:BLOCK, :] = kext_ref[:, ts:ts + BLOCK, :]
    vext_ref[:, 0:BLOCK, :] = vext_ref[:, ts:ts + BLOCK, :]

    o_ref[...] = x + _dot(oacc_ref[...], wo_ref[...]) + bo_ref[...]


def _attn_mixer(x, g, w_qkv, b_qkv, q_gain, k_gain, sinks, w_o, b_o, rel_bias):
    bsz, s, _ = x.shape
    ts = MIX_TOKENS
    assert s % ts == 0 and ts % BLOCK == 0
    qkv_w = (N_HEADS + 2 * N_KV_HEADS) * HEAD_DIM
    kv_w = N_KV_HEADS * HEAD_DIM
    buckets, in_band = _band_tables()
    qg = jnp.tile(q_gain, N_HEADS).reshape(1, D_MODEL)
    kg = jnp.tile(k_gain, N_KV_HEADS).reshape(1, kv_w)
    smem = pl.BlockSpec(memory_space=pltpu.SMEM)
    return pl.pallas_call(
        _attn_body,
        grid=(bsz, s // ts),
        in_specs=[
            smem,
            smem,
            pl.BlockSpec((None, ts, D_MODEL), lambda b, i: (b, i, 0)),
            _const_spec((1, D_MODEL)),
            _const_spec((D_MODEL, qkv_w)),
            _const_spec((1, qkv_w)),
            _const_spec((1, D_MODEL)),
            _const_spec((1, kv_w)),
            _const_spec((BLOCK, 2 * BLOCK)),
            _const_spec((BLOCK, 2 * BLOCK)),
            _const_spec((D_MODEL, D_MODEL)),
            _const_spec((1, D_MODEL)),
        ],
        out_specs=pl.BlockSpec((None, ts, D_MODEL), lambda b, i: (b, i, 0)),
        out_shape=jax.ShapeDtypeStruct(x.shape, F32),
        scratch_shapes=[
            pltpu.VMEM((2, N_KV_HEADS, PAIRS_PER_KV * BLOCK, 4 * BLOCK), F32),
            pltpu.VMEM((ts, D_MODEL), BF16),
            pltpu.VMEM((2 * N_KV_HEADS, ts + BLOCK, kv_w), BF16),
            pltpu.VMEM((2 * N_KV_HEADS, ts + BLOCK, kv_w), BF16),
            pltpu.VMEM((ts, D_MODEL), BF16),
        ],
        compiler_params=_compiler_params(2),
        name="attn_mixer",
    )(sinks, rel_bias, x, g.reshape(1, D_MODEL), w_qkv.astype(BF16), b_qkv.reshape(1, qkv_w), qg, kg,
      jnp.asarray(buckets), jnp.asarray(in_band), w_o.astype(BF16), b_o.reshape(1, D_MODEL))


def kernel(x, ffn1_norm, ffn1_w_in, ffn1_w_out, mix_norm, ffn2_norm, ffn2_w_in, ffn2_w_out, conv_w_in, conv_w,
           conv_w_out, pool_w, pool_b, pool_scale, attn_w_qkv, attn_b_qkv, attn_q_norm, attn_k_norm, attn_sinks,
           attn_w_o, attn_b_o, rel_bias):
    bsz, s, d = x.shape
    ffn1 = (ffn1_norm.reshape(DEPTH, 1, d), ffn1_w_in, ffn1_w_out)
    ffn2 = (ffn2_norm.reshape(DEPTH, 1, d), ffn2_w_in, ffn2_w_out)

    def ffn(xx, params, layer):
        return _ffn(xx.reshape(bsz * s, d), *params, layer).reshape(bsz, s, d)

    for i in range(DEPTH):
        x = ffn(x, ffn1, i)
        kind, j = i % N_MIXERS, i // N_MIXERS
        if kind == 0:
            x = _conv_mixer(x, mix_norm[i], conv_w_in, conv_w, conv_w_out, j)
        elif kind == 1:
            x = _pool_mixer(x, mix_norm[i], pool_w[j], pool_b[j], pool_scale[j])
        else:
            x = _attn_mixer(x, mix_norm[i], attn_w_qkv[j], attn_b_qkv[j], attn_q_norm[j], attn_k_norm[j],
                            attn_sinks[j], attn_w_o[j], attn_b_o[j], rel_bias)
        x = ffn(x, ffn2, i)
    return x
```

```python
import functools
import math

import numpy as np
import jax
import jax.numpy as jnp
from jax import lax
from jax.experimental import pallas as pl
from jax.experimental.pallas import tpu as pltpu

D_MODEL = 1024
D_FF = 2816
DEPTH = 4
N_MIXERS = 3
CONV_WIDTH = 3
POOL_WINDOWS = (2, 4, 8, 16)
POOL_CH = D_MODEL // len(POOL_WINDOWS)
HEAD_DIM = 64
N_HEADS = D_MODEL // HEAD_DIM
N_KV_HEADS = 2
GQA_GROUP = N_HEADS // N_KV_HEADS
WINDOW = 128
BLOCK = 128
NUM_BUCKETS = 32
MAX_EXACT = NUM_BUCKETS // 2
MAX_DISTANCE = 128
EPS = 1e-6

LANES = 128
SUBLANES = 8
MASK_VALUE = -1e30
VMEM_LIMIT_BYTES = 56 * 1024 * 1024

FFN_TOKENS = 1024
FFN_CHUNK = 256
FFN_HEAD_ROWS = 256
FETCH_SLOTS = 4
FFN_WIN_ROWS = 64
FFN_WOUT_ROWS = 128
MIX_TOKENS = 1024
CONV_TOKENS = 1024
CONV_SUB = 512
CONV_W_ROWS = 128
POOL_HALO = 16
HEAD_PAIRS = N_HEADS // 2
PAIRS_PER_KV = HEAD_PAIRS // N_KV_HEADS

F32 = jnp.float32
BF16 = jnp.bfloat16


def _rmsnorm_bf16(x, g):
    ms = jnp.mean(x * x, axis=-1, keepdims=True)
    return (x * lax.rsqrt(ms + EPS) * g).astype(BF16)


def _dot(a, b):
    return jnp.dot(a, b, preferred_element_type=F32)


def _const_spec(shape):
    return pl.BlockSpec(shape, lambda *_: (0,) * len(shape), pipeline_mode=pl.Buffered(1))


def _layer_spec(shape, layer):
    return pl.BlockSpec((None,) + shape, lambda *_: (layer,) + (0,) * len(shape), pipeline_mode=pl.Buffered(1))


def _fetch_weights_bf16(streams):
    def copy(stream, k):
        src, _, stage_ref, sem = stream
        slots, rows = stage_ref.shape[0], stage_ref.shape[1]
        return pltpu.make_async_copy(src.at[pl.ds(k * rows, rows)], stage_ref.at[k % slots], sem.at[k % slots])

    counts = []
    for stream in streams:
        src, _, stage_ref, _ = stream
        slots, rows = stage_ref.shape[0], stage_ref.shape[1]
        assert src.shape[0] % rows == 0
        counts.append(src.shape[0] // rows)
        for k in range(min(slots, counts[-1])):
            copy(stream, k).start()
    for k in range(max(counts)):
        for stream, n in zip(streams, counts):
            if k < n:
                _, dst_ref, stage_ref, _ = stream
                slots, rows = stage_ref.shape[0], stage_ref.shape[1]
                copy(stream, k).wait()
                dst_ref[k * rows:(k + 1) * rows, :] = stage_ref[k % slots].astype(BF16)
                if k + slots < n:
                    copy(stream, k + slots).start()


def _compiler_params(n_grid_dims):
    return pltpu.CompilerParams(
        dimension_semantics=("arbitrary",) * n_grid_dims,
        vmem_limit_bytes=VMEM_LIMIT_BYTES,
    )


def _ffn_body(x_ref, g_ref, win_hbm, wout_hbm, o_ref, win_ref, wout_ref, stage_in, stage_out, sem_in, sem_out, *,
              layer):
    @pl.when(pl.program_id(0) == 0)
    def _():
        _fetch_weights_bf16([(win_hbm.at[layer], win_ref, stage_in, sem_in),
                             (wout_hbm.at[layer], wout_ref, stage_out, sem_out)])

    x = x_ref[...]
    h = _rmsnorm_bf16(x, g_ref[...])

    def proj(c, col0):
        w = win_ref[:, col0 + c * FFN_CHUNK:col0 + (c + 1) * FFN_CHUNK]
        if c > 0:
            return _dot(h, w)
        return jnp.concatenate([_dot(h[r:r + FFN_HEAD_ROWS], w) for r in range(0, FFN_TOKENS, FFN_HEAD_ROWS)], axis=0)

    acc = None
    for c in range(D_FF // FFN_CHUNK):
        lo = c * FFN_CHUNK
        gate = proj(c, 0)
        up = proj(c, D_FF)
        act = (gate * jax.nn.sigmoid(gate) * up).astype(BF16)
        part = _dot(act, wout_ref[lo:lo + FFN_CHUNK, :])
        acc = part if acc is None else acc + part
    o_ref[...] = x + 0.5 * acc


def _ffn(x2, g, w_in, w_out, layer):
    t = x2.shape[0]
    assert t % FFN_TOKENS == 0
    return pl.pallas_call(
        functools.partial(_ffn_body, layer=layer),
        grid=(t // FFN_TOKENS,),
        in_specs=[
            pl.BlockSpec((FFN_TOKENS, D_MODEL), lambda i: (i, 0)),
            _layer_spec((1, D_MODEL), layer),
            pl.BlockSpec(memory_space=pl.ANY),
            pl.BlockSpec(memory_space=pl.ANY),
        ],
        out_specs=pl.BlockSpec((FFN_TOKENS, D_MODEL), lambda i: (i, 0)),
        out_shape=jax.ShapeDtypeStruct((t, D_MODEL), F32),
        scratch_shapes=[
            pltpu.VMEM((D_MODEL, 2 * D_FF), BF16),
            pltpu.VMEM((D_FF, D_MODEL), BF16),
            pltpu.VMEM((FETCH_SLOTS, FFN_WIN_ROWS, 2 * D_FF), F32),
            pltpu.VMEM((FETCH_SLOTS, FFN_WOUT_ROWS, D_MODEL), F32),
            pltpu.SemaphoreType.DMA((FETCH_SLOTS,)),
            pltpu.SemaphoreType.DMA((FETCH_SLOTS,)),
        ],
        compiler_params=_compiler_params(1),
        name="ffn",
    )(x2, g, w_in, w_out)


def _conv_body(x_ref, g_ref, win_hbm, cw_ref, wout_hbm, o_ref, prev_ref, win_ref, wout_ref, stage_in, stage_out,
               sem_in, sem_out, *, layer):
    ts = x_ref.shape[0]

    @pl.when(jnp.logical_and(pl.program_id(0) == 0, pl.program_id(1) == 0))
    def _():
        _fetch_weights_bf16([(win_hbm.at[layer], win_ref, stage_in, sem_in),
                             (wout_hbm.at[layer], wout_ref, stage_out, sem_out)])

    @pl.when(pl.program_id(1) == 0)
    def _():
        prev_ref[...] = jnp.zeros_like(prev_ref)

    w0 = cw_ref[0:1, :]
    w1 = cw_ref[1:2, :]
    w2 = cw_ref[2:3, :]
    row = lax.broadcasted_iota(jnp.int32, (SUBLANES, D_MODEL), 0)
    prev = prev_ref[...]
    for r in range(0, ts, CONV_SUB):
        x = x_ref[r:r + CONV_SUB, :]
        h = _rmsnorm_bf16(x, g_ref[...])
        c = _dot(h, win_ref[:, D_MODEL:2 * D_MODEL])
        v = _dot(h, win_ref[:, 2 * D_MODEL:3 * D_MODEL])
        z = c * v
        z1 = pltpu.roll(z, 1, 0)
        z2 = pltpu.roll(z, 2, 0)
        head1 = jnp.where(row < 1, pltpu.roll(prev, 1, 0), z1[:SUBLANES])
        head2 = jnp.where(row < 2, pltpu.roll(prev, 2, 0), z2[:SUBLANES])
        conv_head = w0 * head2 + w1 * head1 + w2 * z[:SUBLANES]
        conv_rest = w0 * z2[SUBLANES:] + w1 * z1[SUBLANES:] + w2 * z[SUBLANES:]
        conv = jnp.concatenate([conv_head, conv_rest], axis=0)
        prev = z[CONV_SUB - SUBLANES:, :]

        b = _dot(h, win_ref[:, 0:D_MODEL])
        u = (b * conv).astype(BF16)
        o_ref[r:r + CONV_SUB, :] = x + _dot(u, wout_ref[...])
    prev_ref[...] = prev


def _conv_mixer(x, g, w_in, conv_w, w_out, layer):
    bsz, s, _ = x.shape
    ts = CONV_TOKENS
    assert s % ts == 0
    return pl.pallas_call(
        functools.partial(_conv_body, layer=layer),
        grid=(bsz, s // ts),
        in_specs=[
            pl.BlockSpec((None, ts, D_MODEL), lambda b, i: (b, i, 0)),
            _const_spec((1, D_MODEL)),
            pl.BlockSpec(memory_space=pl.ANY),
            _layer_spec((CONV_WIDTH, D_MODEL), layer),
            pl.BlockSpec(memory_space=pl.ANY),
        ],
        out_specs=pl.BlockSpec((None, ts, D_MODEL), lambda b, i: (b, i, 0)),
        out_shape=jax.ShapeDtypeStruct(x.shape, F32),
        scratch_shapes=[
            pltpu.VMEM((SUBLANES, D_MODEL), F32),
            pltpu.VMEM((D_MODEL, 3 * D_MODEL), BF16),
            pltpu.VMEM((D_MODEL, D_MODEL), BF16),
            pltpu.VMEM((FETCH_SLOTS, CONV_W_ROWS, 3 * D_MODEL), F32),
            pltpu.VMEM((FETCH_SLOTS, CONV_W_ROWS, D_MODEL), F32),
            pltpu.SemaphoreType.DMA((FETCH_SLOTS,)),
            pltpu.SemaphoreType.DMA((FETCH_SLOTS,)),
        ],
        compiler_params=_compiler_params(2),
        name="conv_mixer",
    )(x, g.reshape(1, D_MODEL), w_in, conv_w, w_out)


def _pool_body(x_ref, g_ref, w_ref, b_ref, sc_ref, o_ref, prev_ref):
    ts = x_ref.shape[0]
    i = pl.program_id(1)

    @pl.when(i == 0)
    def _():
        prev_ref[...] = jnp.zeros_like(prev_ref)

    x = x_ref[...]
    ms = jnp.mean(x * x, axis=-1, keepdims=True)
    h = x * lax.rsqrt(ms + EPS) * g_ref[...]
    prev = prev_ref[...]
    prev_ref[...] = h[ts - POOL_HALO:, :]

    pos = i * ts + lax.broadcasted_iota(jnp.int32, (ts, POOL_CH), 0)
    ys = []
    for gi, win in enumerate(POOL_WINDOWS):
        lo = gi * POOL_CH
        hg = h[:, lo:lo + POOL_CH]
        ssum = jnp.concatenate([prev[:, lo:lo + POOL_CH], hg], axis=0)
        shift = 1
        while shift < win:
            ssum = ssum + pltpu.roll(ssum, shift, 0)
            shift *= 2
        cnt = jnp.minimum(pos + 1, win).astype(F32)
        diff = ssum[POOL_HALO:] / cnt - hg
        ys.append(_dot(diff.astype(BF16), w_ref[gi]))
    y = jnp.concatenate(ys, axis=1)
    o_ref[...] = x + (y + b_ref[...]) * sc_ref[...]


def _pool_mixer(x, g, w, bias, scale):
    bsz, s, _ = x.shape
    ts = CONV_TOKENS
    assert s % ts == 0
    ng = len(POOL_WINDOWS)
    return pl.pallas_call(
        _pool_body,
        grid=(bsz, s // ts),
        in_specs=[
            pl.BlockSpec((None, ts, D_MODEL), lambda b, i: (b, i, 0)),
            _const_spec((1, D_MODEL)),
            _const_spec((ng, POOL_CH, POOL_CH)),
            _const_spec((1, D_MODEL)),
            _const_spec((1, D_MODEL)),
        ],
        out_specs=pl.BlockSpec((None, ts, D_MODEL), lambda b, i: (b, i, 0)),
        out_shape=jax.ShapeDtypeStruct(x.shape, F32),
        scratch_shapes=[pltpu.VMEM((POOL_HALO, D_MODEL), F32)],
        compiler_params=_compiler_params(2),
        name="pool_mixer",
    )(x, g.reshape(1, D_MODEL), w.astype(BF16), bias.reshape(1, D_MODEL), scale.reshape(1, D_MODEL))


def _t5_causal_buckets(n):
    nf = np.maximum(n, 1).astype(np.float32)
    large = MAX_EXACT + (np.log(nf / MAX_EXACT) / math.log(MAX_DISTANCE / MAX_EXACT)
                         * (NUM_BUCKETS - MAX_EXACT)).astype(np.int32)
    large = np.minimum(large, NUM_BUCKETS - 1)
    return np.where(n < MAX_EXACT, n, large).astype(np.int32)


def _band_tables():
    qi = np.arange(BLOCK)[:, None]
    ki = np.arange(2 * BLOCK)[None, :]
    dist = qi + BLOCK - ki
    in_band = ((dist >= 0) & (dist < WINDOW)).astype(np.int32)
    assert not in_band[:, 0].any()
    return _t5_causal_buckets(dist), in_band


def _group_sumsq(x, ones_blk):
    sq = x * x
    hi = sq.astype(BF16)
    lo = (sq - hi.astype(F32)).astype(BF16)
    n = x.shape[1]
    w = ones_blk.shape[0]
    outs = [_dot(hi[:, s:s + w], ones_blk) + _dot(lo[:, s:s + w], ones_blk) for s in range(0, n, w)]
    return outs[0] if len(outs) == 1 else jnp.concatenate(outs, axis=1)


def _attn_body(sink_ref, relb_ref, x_ref, g_ref, wqkv_ref, bqkv_ref, qg_ref, kg_ref, bkt_ref, band_ref,
               wo_ref, bo_ref, o_ref, bias_ref, q_ref, kext_ref, vext_ref, oacc_ref):
    ts = x_ref.shape[0]
    nblk = ts // BLOCK
    first_tile = pl.program_id(1) == 0
    kv_w = N_KV_HEADS * HEAD_DIM

    @pl.when(jnp.logical_and(pl.program_id(0) == 0, pl.program_id(1) == 0))
    def _():
        bkt = bkt_ref[...]
        band = band_ref[...] > 0
        key = lax.broadcasted_iota(jnp.int32, (BLOCK, 2 * BLOCK), 1)
        has_prev = key >= BLOCK
        sink_slot = key == 0
        for kv in range(N_KV_HEADS):
            for pair in range(PAIRS_PER_KV):
                for par in range(2):
                    head = kv * GQA_GROUP + 2 * pair + par

                    def pick(b, acc):
                        return jnp.where(bkt == b, relb_ref[b, head], acc)

                    vals = lax.fori_loop(0, NUM_BUCKETS, pick, jnp.zeros((BLOCK, 2 * BLOCK), F32))
                    vals = jnp.where(band, vals, MASK_VALUE)
                    sink = sink_ref[head]
                    rows = slice(pair * BLOCK, (pair + 1) * BLOCK)
                    cols = slice(par * 2 * BLOCK, (par + 1) * 2 * BLOCK)
                    bias_ref[0, kv, rows, cols] = jnp.where(sink_slot, sink, vals)
                    bias_ref[1, kv, rows, cols] = jnp.where(sink_slot, sink, jnp.where(has_prev, vals, MASK_VALUE))

    @pl.when(first_tile)
    def _():
        kext_ref[:, 0:BLOCK, :] = jnp.zeros((2 * N_KV_HEADS, BLOCK, kv_w), BF16)
        low_prev = lax.broadcasted_iota(jnp.int32, (BLOCK, kv_w), 1) < HEAD_DIM
        for c in range(N_KV_HEADS):
            vext_ref[2 * c, 0:BLOCK, :] = jnp.where(low_prev, 0.0, 1.0).astype(BF16)
            vext_ref[2 * c + 1, 0:BLOCK, :] = jnp.where(low_prev, 1.0, 0.0).astype(BF16)

    x = x_ref[...]
    h = _rmsnorm_bf16(x, g_ref[...])

    q = _dot(h, wqkv_ref[:, 0:D_MODEL]) + bqkv_ref[:, 0:D_MODEL]
    head_shift = HEAD_DIM.bit_length() - 1
    r4 = lax.broadcasted_iota(jnp.int32, (2 * LANES, 2 * LANES), 0) >> head_shift
    c4 = lax.broadcasted_iota(jnp.int32, (2 * LANES, 2 * LANES), 1) >> head_shift
    ones4 = jnp.where(r4 == c4, 1.0, 0.0).astype(BF16)
    qss = _group_sumsq(q, ones4)
    q_ref[...] = (q * lax.rsqrt(qss + HEAD_DIM * EPS) * qg_ref[...]).astype(BF16)

    kvp = _dot(h, wqkv_ref[:, D_MODEL:D_MODEL + 2 * kv_w]) + bqkv_ref[:, D_MODEL:D_MODEL + 2 * kv_w]
    k = kvp[:, 0:kv_w]
    v = kvp[:, kv_w:2 * kv_w]
    ones2 = ones4[0:kv_w, 0:kv_w]
    kss = _group_sumsq(k, ones2)
    kn = k * lax.rsqrt(kss * (1.0 / HEAD_DIM) + EPS) * kg_ref[...]

    lane = lax.broadcasted_iota(jnp.int32, (ts, kv_w), 1)
    low = lane < HEAD_DIM
    kn_sw = pltpu.roll(kn, HEAD_DIM, 1)
    v_sw = pltpu.roll(v, HEAD_DIM, 1)
    for c in range(N_KV_HEADS):
        for p in range(2):
            half = low if p == 0 else jnp.logical_not(low)
            ksrc = kn if c == p else kn_sw
            vsrc = v if c == p else v_sw
            kext_ref[2 * c + p, BLOCK:, :] = jnp.where(half, ksrc, 0.0).astype(BF16)
            vext_ref[2 * c + p, BLOCK:, :] = jnp.where(half, vsrc, 1.0).astype(BF16)

    low_blk = lax.broadcasted_iota(jnp.int32, (BLOCK, LANES), 1) < HEAD_DIM
    bf16_rows = 2 * SUBLANES
    slot0 = lax.broadcasted_iota(jnp.int32, (bf16_rows, LANES), 0) == 0
    low0 = lax.broadcasted_iota(jnp.int32, (bf16_rows, LANES), 1) < HEAD_DIM
    zeros0 = jnp.zeros((bf16_rows, LANES), BF16)

    def clear_slot0(a, mask):
        return jnp.concatenate([jnp.where(mask, zeros0, a[:bf16_rows]), a[bf16_rows:]], axis=0)

    for j in range(nblk):
        r0 = j * BLOCK
        variant = jnp.where(first_tile, 1, 0) if j == 0 else 0
        for c in range(N_KV_HEADS):
            qs = jnp.concatenate(
                [q_ref[r0:r0 + BLOCK, (c * PAIRS_PER_KV + i) * LANES:(c * PAIRS_PER_KV + i + 1) * LANES]
                 for i in range(PAIRS_PER_KV)], axis=0)
            ox = []
            for p in range(2):
                v_half = low0 if p == 0 else jnp.logical_not(low0)
                kx = clear_slot0(kext_ref[2 * c + p, r0:r0 + 2 * BLOCK, :], slot0)
                vx = clear_slot0(vext_ref[2 * c + p, r0:r0 + 2 * BLOCK, :], jnp.logical_and(slot0, v_half))
                s = lax.dot_general(qs, kx, (((1,), (1,)), ((), ())), preferred_element_type=F32)
                s = s + bias_ref[variant, c, :, p * 2 * BLOCK:(p + 1) * 2 * BLOCK]
                pes = []
                for i in range(PAIRS_PER_KV):
                    si = s[i * BLOCK:(i + 1) * BLOCK]
                    m = jnp.max(si, axis=-1, keepdims=True)
                    pes.append(jnp.exp(si - m).astype(BF16))
                ox.append(_dot(jnp.concatenate(pes, axis=0), vx))
            for i in range(PAIRS_PER_KV):
                pair = c * PAIRS_PER_KV + i
                ox_e = ox[0][i * BLOCK:(i + 1) * BLOCK]
                ox_o = ox[1][i * BLOCK:(i + 1) * BLOCK]
                num = jnp.where(low_blk, ox_e, ox_o)
                den = pltpu.roll(jnp.where(low_blk, ox_o, ox_e), HEAD_DIM, 1)
                oacc_ref[r0:r0 + BLOCK, pair * LANES:(pair + 1) * LANES] = (num / den).astype(BF16)

    kext_ref[:, 0:BLOCK, :] = kext_ref[:, ts:ts + BLOCK, :]
    vext_ref[:, 0:BLOCK, :] = vext_ref[:, ts:ts + BLOCK, :]

    o_ref[...] = x + _dot(oacc_ref[...], wo_ref[...]) + bo_ref[...]


def _attn_mixer(x, g, w_qkv, b_qkv, q_gain, k_gain, sinks, w_o, b_o, rel_bias):
    bsz, s, _ = x.shape
    ts = MIX_TOKENS
    assert s % ts == 0 and ts % BLOCK == 0
    qkv_w = (N_HEADS + 2 * N_KV_HEADS) * HEAD_DIM
    kv_w = N_KV_HEADS * HEAD_DIM
    buckets, in_band = _band_tables()
    qg = jnp.tile(q_gain, N_HEADS).reshape(1, D_MODEL)
    kg = jnp.tile(k_gain, N_KV_HEADS).reshape(1, kv_w)
    smem = pl.BlockSpec(memory_space=pltpu.SMEM)
    return pl.pallas_call(
        _attn_body,
        grid=(bsz, s // ts),
        in_specs=[
            smem,
            smem,
            pl.BlockSpec((None, ts, D_MODEL), lambda b, i: (b, i, 0)),
            _const_spec((1, D_MODEL)),
            _const_spec((D_MODEL, qkv_w)),
            _const_spec((1, qkv_w)),
            _const_spec((1, D_MODEL)),
            _const_spec((1, kv_w)),
            _const_spec((BLOCK, 2 * BLOCK)),
            _const_spec((BLOCK, 2 * BLOCK)),
            _const_spec((D_MODEL, D_MODEL)),
            _const_spec((1, D_MODEL)),
        ],
        out_specs=pl.BlockSpec((None, ts, D_MODEL), lambda b, i: (b, i, 0)),
        out_shape=jax.ShapeDtypeStruct(x.shape, F32),
        scratch_shapes=[
            pltpu.VMEM((2, N_KV_HEADS, PAIRS_PER_KV * BLOCK, 4 * BLOCK), F32),
            pltpu.VMEM((ts, D_MODEL), BF16),
            pltpu.VMEM((2 * N_KV_HEADS, ts + BLOCK, kv_w), BF16),
            pltpu.VMEM((2 * N_KV_HEADS, ts + BLOCK, kv_w), BF16),
            pltpu.VMEM((ts, D_MODEL), BF16),
        ],
        compiler_params=_compiler_params(2),
        name="attn_mixer",
    )(sinks, rel_bias, x, g.reshape(1, D_MODEL), w_qkv.astype(BF16), b_qkv.reshape(1, qkv_w), qg, kg,
      jnp.asarray(buckets), jnp.asarray(in_band), w_o.astype(BF16), b_o.reshape(1, D_MODEL))


def kernel(x, ffn1_norm, ffn1_w_in, ffn1_w_out, mix_norm, ffn2_norm, ffn2_w_in, ffn2_w_out, conv_w_in, conv_w,
           conv_w_out, pool_w, pool_b, pool_scale, attn_w_qkv, attn_b_qkv, attn_q_norm, attn_k_norm, attn_sinks,
           attn_w_o, attn_b_o, rel_bias):
    bsz, s, d = x.shape
    ffn1 = (ffn1_norm.reshape(DEPTH, 1, d), ffn1_w_in, ffn1_w_out)
    ffn2 = (ffn2_norm.reshape(DEPTH, 1, d), ffn2_w_in, ffn2_w_out)

    def ffn(xx, params, layer):
        return _ffn(xx.reshape(bsz * s, d), *params, layer).reshape(bsz, s, d)

    for i in range(DEPTH):
        x = ffn(x, ffn1, i)
        kind, j = i % N_MIXERS, i // N_MIXERS
        if kind == 0:
            x = _conv_mixer(x, mix_norm[i], conv_w_in, conv_w, conv_w_out, j)
        elif kind == 1:
            x = _pool_mixer(x, mix_norm[i], pool_w[j], pool_b[j], pool_scale[j])
        else:
            x = _attn_mixer(x, mix_norm[i], attn_w_qkv[j], attn_b_qkv[j], attn_q_norm[j], attn_k_norm[j],
                            attn_sinks[j], attn_w_o[j], attn_b_o[j], rel_bias)
        x = ffn(x, ffn2, i)
    return x
```

```python
import functools
import math

import numpy as np
import jax
import jax.numpy as jnp
from jax import lax
from jax.experimental import pallas as pl
from jax.experimental.pallas import tpu as pltpu

D_MODEL = 1024
D_FF = 2816
DEPTH = 4
N_MIXERS = 3
CONV_WIDTH = 3
POOL_WINDOWS = (2, 4, 8, 16)
POOL_CH = D_MODEL // len(POOL_WINDOWS)
HEAD_DIM = 64
N_HEADS = D_MODEL // HEAD_DIM
N_KV_HEADS = 2
GQA_GROUP = N_HEADS // N_KV_HEADS
WINDOW = 128
BLOCK = 128
NUM_BUCKETS = 32
MAX_EXACT = NUM_BUCKETS // 2
MAX_DISTANCE = 128
EPS = 1e-6

LANES = 128
SUBLANES = 8
MASK_VALUE = -1e30
VMEM_LIMIT_BYTES = 56 * 1024 * 1024

FFN_TOKENS = 1024
FFN_CHUNK = 256
FFN_SUB = FFN_TOKENS // 2
FFN_PIECE = 64
FETCH_SLOTS = 4
FFN_WIN_ROWS = 64
FFN_WOUT_ROWS = 128
MIX_TOKENS = 1024
CONV_TOKENS = 1024
CONV_SUB = 512
CONV_W_ROWS = 128
POOL_HALO = 16
HEAD_PAIRS = N_HEADS // 2
PAIRS_PER_KV = HEAD_PAIRS // N_KV_HEADS

F32 = jnp.float32
BF16 = jnp.bfloat16


def _rmsnorm_bf16(x, g):
    ms = jnp.mean(x * x, axis=-1, keepdims=True)
    return (x * lax.rsqrt(ms + EPS) * g).astype(BF16)


def _dot(a, b):
    return jnp.dot(a, b, preferred_element_type=F32)


def _const_spec(shape):
    return pl.BlockSpec(shape, lambda *_: (0,) * len(shape), pipeline_mode=pl.Buffered(1))


def _layer_spec(shape, layer):
    return pl.BlockSpec((None,) + shape, lambda *_: (layer,) + (0,) * len(shape), pipeline_mode=pl.Buffered(1))


def _fetch_weights_bf16(streams):
    def copy(stream, k):
        src, _, stage_ref, sem = stream
        slots, rows = stage_ref.shape[0], stage_ref.shape[1]
        return pltpu.make_async_copy(src.at[pl.ds(k * rows, rows)], stage_ref.at[k % slots], sem.at[k % slots])

    counts = []
    for stream in streams:
        src, _, stage_ref, _ = stream
        slots, rows = stage_ref.shape[0], stage_ref.shape[1]
        assert src.shape[0] % rows == 0
        counts.append(src.shape[0] // rows)
        for k in range(min(slots, counts[-1])):
            copy(stream, k).start()
    for k in range(max(counts)):
        for stream, n in zip(streams, counts):
            if k < n:
                _, dst_ref, stage_ref, _ = stream
                slots, rows = stage_ref.shape[0], stage_ref.shape[1]
                copy(stream, k).wait()
                dst_ref[k * rows:(k + 1) * rows, :] = stage_ref[k % slots].astype(BF16)
                if k + slots < n:
                    copy(stream, k + slots).start()


def _compiler_params(n_grid_dims):
    return pltpu.CompilerParams(
        dimension_semantics=("arbitrary",) * n_grid_dims,
        vmem_limit_bytes=VMEM_LIMIT_BYTES,
    )


def _ffn_body(x_ref, xn_ref, g_ref, win_hbm, wout_hbm, o_ref, h_ref, win_ref, wout_ref, stage_in, stage_out, sem_in,
              sem_out, *, layer):
    g = g_ref[...]

    @pl.when(pl.program_id(0) == 0)
    def _():
        _fetch_weights_bf16([(win_hbm.at[layer], win_ref, stage_in, sem_in),
                             (wout_hbm.at[layer], wout_ref, stage_out, sem_out)])
        h_ref[0] = _rmsnorm_bf16(x_ref[0:FFN_SUB, :], g)

    def norm_piece(src_ref, src_row, slot, k):
        r = k * FFN_PIECE
        hk = _rmsnorm_bf16(src_ref[src_row + r:src_row + r + FFN_PIECE, :], g)
        h_ref[slot, r:r + FFN_PIECE, :] = hk
        u = pltpu.bitcast(hk, jnp.uint32)
        t = u[0:SUBLANES]
        for rr in range(SUBLANES, u.shape[0], SUBLANES):
            t = t | u[rr:rr + SUBLANES]
        z = t[:, 0:LANES]
        for cc in range(LANES, D_MODEL, LANES):
            z = z | t[:, cc:cc + LANES]
        return (z >> 16) >> 16

    def attach(a, z):
        head = pltpu.bitcast(pltpu.bitcast(a[0:SUBLANES, 0:LANES], jnp.uint32) | z, F32)
        top = jnp.concatenate([head, a[0:SUBLANES, LANES:]], axis=1)
        return jnp.concatenate([top, a[SUBLANES:]], axis=0)

    for half in range(2):
        slot, other = half, 1 - half
        src_ref, src_row = (x_ref, FFN_SUB) if half == 0 else (xn_ref, 0)
        acc = None
        for c in range(D_FF // FFN_CHUNK):
            lo = c * FFN_CHUNK
            gate = _dot(h_ref[slot], win_ref[:, lo:lo + FFN_CHUNK])
            up = _dot(h_ref[slot], win_ref[:, D_FF + lo:D_FF + lo + FFN_CHUNK])
            if 1 <= c <= FFN_SUB // FFN_PIECE:
                up = attach(up, norm_piece(src_ref, src_row, other, c - 1))
            act = (gate * jax.nn.sigmoid(gate) * up).astype(BF16)
            part = _dot(act, wout_ref[lo:lo + FFN_CHUNK, :])
            acc = part if acc is None else acc + part
        rows = slice(half * FFN_SUB, (half + 1) * FFN_SUB)
        o_ref[rows, :] = x_ref[rows, :] + 0.5 * acc


def _ffn(x2, g, w_in, w_out, layer):
    t = x2.shape[0]
    assert t % FFN_TOKENS == 0
    n_sub = t // FFN_SUB
    return pl.pallas_call(
        functools.partial(_ffn_body, layer=layer),
        grid=(t // FFN_TOKENS,),
        in_specs=[
            pl.BlockSpec((FFN_TOKENS, D_MODEL), lambda i: (i, 0)),
            pl.BlockSpec((FFN_SUB, D_MODEL), lambda i: (jnp.minimum(2 * i + 2, n_sub - 1), 0)),
            _layer_spec((1, D_MODEL), layer),
            pl.BlockSpec(memory_space=pl.ANY),
            pl.BlockSpec(memory_space=pl.ANY),
        ],
        out_specs=pl.BlockSpec((FFN_TOKENS, D_MODEL), lambda i: (i, 0)),
        out_shape=jax.ShapeDtypeStruct((t, D_MODEL), F32),
        scratch_shapes=[
            pltpu.VMEM((2, FFN_SUB, D_MODEL), BF16),
            pltpu.VMEM((D_MODEL, 2 * D_FF), BF16),
            pltpu.VMEM((D_FF, D_MODEL), BF16),
            pltpu.VMEM((FETCH_SLOTS, FFN_WIN_ROWS, 2 * D_FF), F32),
            pltpu.VMEM((FETCH_SLOTS, FFN_WOUT_ROWS, D_MODEL), F32),
            pltpu.SemaphoreType.DMA((FETCH_SLOTS,)),
            pltpu.SemaphoreType.DMA((FETCH_SLOTS,)),
        ],
        compiler_params=_compiler_params(1),
        name="ffn",
    )(x2, x2, g, w_in, w_out)


def _conv_body(x_ref, g_ref, win_hbm, cw_ref, wout_hbm, o_ref, prev_ref, win_ref, wout_ref, stage_in, stage_out,
               sem_in, sem_out, *, layer):
    ts = x_ref.shape[0]

    @pl.when(jnp.logical_and(pl.program_id(0) == 0, pl.program_id(1) == 0))
    def _():
        _fetch_weights_bf16([(win_hbm.at[layer], win_ref, stage_in, sem_in),
                             (wout_hbm.at[layer], wout_ref, stage_out, sem_out)])

    @pl.when(pl.program_id(1) == 0)
    def _():
        prev_ref[...] = jnp.zeros_like(prev_ref)

    w0 = cw_ref[0:1, :]
    w1 = cw_ref[1:2, :]
    w2 = cw_ref[2:3, :]
    row = lax.broadcasted_iota(jnp.int32, (SUBLANES, D_MODEL), 0)
    prev = prev_ref[...]
    for r in range(0, ts, CONV_SUB):
        x = x_ref[r:r + CONV_SUB, :]
        h = _rmsnorm_bf16(x, g_ref[...])
        c = _dot(h, win_ref[:, D_MODEL:2 * D_MODEL])
        v = _dot(h, win_ref[:, 2 * D_MODEL:3 * D_MODEL])
        z = c * v
        z1 = pltpu.roll(z, 1, 0)
        z2 = pltpu.roll(z, 2, 0)
        head1 = jnp.where(row < 1, pltpu.roll(prev, 1, 0), z1[:SUBLANES])
        head2 = jnp.where(row < 2, pltpu.roll(prev, 2, 0), z2[:SUBLANES])
        conv_head = w0 * head2 + w1 * head1 + w2 * z[:SUBLANES]
        conv_rest = w0 * z2[SUBLANES:] + w1 * z1[SUBLANES:] + w2 * z[SUBLANES:]
        conv = jnp.concatenate([conv_head, conv_rest], axis=0)
        prev = z[CONV_SUB - SUBLANES:, :]

        b = _dot(h, win_ref[:, 0:D_MODEL])
        u = (b * conv).astype(BF16)
        o_ref[r:r + CONV_SUB, :] = x + _dot(u, wout_ref[...])
    prev_ref[...] = prev


def _conv_mixer(x, g, w_in, conv_w, w_out, layer):
    bsz, s, _ = x.shape
    ts = CONV_TOKENS
    assert s % ts == 0
    return pl.pallas_call(
        functools.partial(_conv_body, layer=layer),
        grid=(bsz, s // ts),
        in_specs=[
            pl.BlockSpec((None, ts, D_MODEL), lambda b, i: (b, i, 0)),
            _const_spec((1, D_MODEL)),
            pl.BlockSpec(memory_space=pl.ANY),
            _layer_spec((CONV_WIDTH, D_MODEL), layer),
            pl.BlockSpec(memory_space=pl.ANY),
        ],
        out_specs=pl.BlockSpec((None, ts, D_MODEL), lambda b, i: (b, i, 0)),
        out_shape=jax.ShapeDtypeStruct(x.shape, F32),
        scratch_shapes=[
            pltpu.VMEM((SUBLANES, D_MODEL), F32),
            pltpu.VMEM((D_MODEL, 3 * D_MODEL), BF16),
            pltpu.VMEM((D_MODEL, D_MODEL), BF16),
            pltpu.VMEM((FETCH_SLOTS, CONV_W_ROWS, 3 * D_MODEL), F32),
            pltpu.VMEM((FETCH_SLOTS, CONV_W_ROWS, D_MODEL), F32),
            pltpu.SemaphoreType.DMA((FETCH_SLOTS,)),
            pltpu.SemaphoreType.DMA((FETCH_SLOTS,)),
        ],
        compiler_params=_compiler_params(2),
        name="conv_mixer",
    )(x, g.reshape(1, D_MODEL), w_in, conv_w, w_out)


def _pool_body(x_ref, g_ref, w_ref, b_ref, sc_ref, o_ref, prev_ref):
    ts = x_ref.shape[0]
    i = pl.program_id(1)

    @pl.when(i == 0)
    def _():
        prev_ref[...] = jnp.zeros_like(prev_ref)

    x = x_ref[...]
    ms = jnp.mean(x * x, axis=-1, keepdims=True)
    h = x * lax.rsqrt(ms + EPS) * g_ref[...]
    prev = prev_ref[...]
    prev_ref[...] = h[ts - POOL_HALO:, :]

    pos = i * ts + lax.broadcasted_iota(jnp.int32, (ts, POOL_CH), 0)
    ys = []
    for gi, win in enumerate(POOL_WINDOWS):
        lo = gi * POOL_CH
        hg = h[:, lo:lo + POOL_CH]
        ssum = jnp.concatenate([prev[:, lo:lo + POOL_CH], hg], axis=0)
        shift = 1
        while shift < win:
            ssum = ssum + pltpu.roll(ssum, shift, 0)
            shift *= 2
        cnt = jnp.minimum(pos + 1, win).astype(F32)
        diff = ssum[POOL_HALO:] / cnt - hg
        ys.append(_dot(diff.astype(BF16), w_ref[gi]))
    y = jnp.concatenate(ys, axis=1)
    o_ref[...] = x + (y + b_ref[...]) * sc_ref[...]


def _pool_mixer(x, g, w, bias, scale):
    bsz, s, _ = x.shape
    ts = CONV_TOKENS
    assert s % ts == 0
    ng = len(POOL_WINDOWS)
    return pl.pallas_call(
        _pool_body,
        grid=(bsz, s // ts),
        in_specs=[
            pl.BlockSpec((None, ts, D_MODEL), lambda b, i: (b, i, 0)),
            _const_spec((1, D_MODEL)),
            _const_spec((ng, POOL_CH, POOL_CH)),
            _const_spec((1, D_MODEL)),
            _const_spec((1, D_MODEL)),
        ],
        out_specs=pl.BlockSpec((None, ts, D_MODEL), lambda b, i: (b, i, 0)),
        out_shape=jax.ShapeDtypeStruct(x.shape, F32),
        scratch_shapes=[pltpu.VMEM((POOL_HALO, D_MODEL), F32)],
        compiler_params=_compiler_params(2),
        name="pool_mixer",
    )(x, g.reshape(1, D_MODEL), w.astype(BF16), bias.reshape(1, D_MODEL), scale.reshape(1, D_MODEL))


def _t5_causal_buckets(n):
    nf = np.maximum(n, 1).astype(np.float32)
    large = MAX_EXACT + (np.log(nf / MAX_EXACT) / math.log(MAX_DISTANCE / MAX_EXACT)
                         * (NUM_BUCKETS - MAX_EXACT)).astype(np.int32)
    large = np.minimum(large, NUM_BUCKETS - 1)
    return np.where(n < MAX_EXACT, n, large).astype(np.int32)


def _band_tables():
    qi = np.arange(BLOCK)[:, None]
    ki = np.arange(2 * BLOCK)[None, :]
    dist = qi + BLOCK - ki
    in_band = ((dist >= 0) & (dist < WINDOW)).astype(np.int32)
    assert not in_band[:, 0].any()
    return _t5_causal_buckets(dist), in_band


def _group_sumsq(x, ones_blk):
    sq = x * x
    hi = sq.astype(BF16)
    lo = (sq - hi.astype(F32)).astype(BF16)
    n = x.shape[1]
    w = ones_blk.shape[0]
    outs = [_dot(hi[:, s:s + w], ones_blk) + _dot(lo[:, s:s + w], ones_blk) for s in range(0, n, w)]
    return outs[0] if len(outs) == 1 else jnp.concatenate(outs, axis=1)


def _attn_body(sink_ref, relb_ref, x_ref, g_ref, wqkv_ref, bqkv_ref, qg_ref, kg_ref, bkt_ref, band_ref,
               wo_ref, bo_ref, o_ref, bias_ref, q_ref, kext_ref, vext_ref, oacc_ref):
    ts = x_ref.shape[0]
    nblk = ts // BLOCK
    first_tile = pl.program_id(1) == 0
    kv_w = N_KV_HEADS * HEAD_DIM

    @pl.when(jnp.logical_and(pl.program_id(0) == 0, pl.program_id(1) == 0))
    def _():
        bkt = bkt_ref[...]
        band = band_ref[...] > 0
        key = lax.broadcasted_iota(jnp.int32, (BLOCK, 2 * BLOCK), 1)
        has_prev = key >= BLOCK
        sink_slot = key == 0
        for kv in range(N_KV_HEADS):
            for pair in range(PAIRS_PER_KV):
                for par in range(2):
                    head = kv * GQA_GROUP + 2 * pair + par

                    def pick(b, acc):
                        return jnp.where(bkt == b, relb_ref[b, head], acc)

                    vals = lax.fori_loop(0, NUM_BUCKETS, pick, jnp.zeros((BLOCK, 2 * BLOCK), F32))
                    vals = jnp.where(band, vals, MASK_VALUE)
                    sink = sink_ref[head]
                    rows = slice(pair * BLOCK, (pair + 1) * BLOCK)
                    cols = slice(par * 2 * BLOCK, (par + 1) * 2 * BLOCK)
                    bias_ref[0, kv, rows, cols] = jnp.where(sink_slot, sink, vals)
                    bias_ref[1, kv, rows, cols] = jnp.where(sink_slot, sink, jnp.where(has_prev, vals, MASK_VALUE))

    @pl.when(first_tile)
    def _():
        kext_ref[:, 0:BLOCK, :] = jnp.zeros((2 * N_KV_HEADS, BLOCK, kv_w), BF16)
        low_prev = lax.broadcasted_iota(jnp.int32, (BLOCK, kv_w), 1) < HEAD_DIM
        for c in range(N_KV_HEADS):
            vext_ref[2 * c, 0:BLOCK, :] = jnp.where(low_prev, 0.0, 1.0).astype(BF16)
            vext_ref[2 * c + 1, 0:BLOCK, :] = jnp.where(low_prev, 1.0, 0.0).astype(BF16)

    x = x_ref[...]
    h = _rmsnorm_bf16(x, g_ref[...])

    q = _dot(h, wqkv_ref[:, 0:D_MODEL]) + bqkv_ref[:, 0:D_MODEL]
    head_shift = HEAD_DIM.bit_length() - 1
    r4 = lax.broadcasted_iota(jnp.int32, (2 * LANES, 2 * LANES), 0) >> head_shift
    c4 = lax.broadcasted_iota(jnp.int32, (2 * LANES, 2 * LANES), 1) >> head_shift
    ones4 = jnp.where(r4 == c4, 1.0, 0.0).astype(BF16)
    qss = _group_sumsq(q, ones4)
    q_ref[...] = (q * lax.rsqrt(qss + HEAD_DIM * EPS) * qg_ref[...]).astype(BF16)

    kvp = _dot(h, wqkv_ref[:, D_MODEL:D_MODEL + 2 * kv_w]) + bqkv_ref[:, D_MODEL:D_MODEL + 2 * kv_w]
    k = kvp[:, 0:kv_w]
    v = kvp[:, kv_w:2 * kv_w]
    ones2 = ones4[0:kv_w, 0:kv_w]
    kss = _group_sumsq(k, ones2)
    kn = k * lax.rsqrt(kss * (1.0 / HEAD_DIM) + EPS) * kg_ref[...]

    lane = lax.broadcasted_iota(jnp.int32, (ts, kv_w), 1)
    low = lane < HEAD_DIM
    kn_sw = pltpu.roll(kn, HEAD_DIM, 1)
    v_sw = pltpu.roll(v, HEAD_DIM, 1)
    for c in range(N_KV_HEADS):
        for p in range(2):
            half = low if p == 0 else jnp.logical_not(low)
            ksrc = kn if c == p else kn_sw
            vsrc = v if c == p else v_sw
            kext_ref[2 * c + p, BLOCK:, :] = jnp.where(half, ksrc, 0.0).astype(BF16)
            vext_ref[2 * c + p, BLOCK:, :] = jnp.where(half, vsrc, 1.0).astype(BF16)

    low_blk = lax.broadcasted_iota(jnp.int32, (BLOCK, LANES), 1) < HEAD_DIM
    bf16_rows = 2 * SUBLANES
    slot0 = lax.broadcasted_iota(jnp.int32, (bf16_rows, LANES), 0) == 0
    low0 = lax.broadcasted_iota(jnp.int32, (bf16_rows, LANES), 1) < HEAD_DIM
    zeros0 = jnp.zeros((bf16_rows, LANES), BF16)

    def clear_slot0(a, mask):
        return jnp.concatenate([jnp.where(mask, zeros0, a[:bf16_rows]), a[bf16_rows:]], axis=0)

    for j in range(nblk):
        r0 = j * BLOCK
        variant = jnp.where(first_tile, 1, 0) if j == 0 else 0
        for c in range(N_KV_HEADS):
            qs = jnp.concatenate(
                [q_ref[r0:r0 + BLOCK, (c * PAIRS_PER_KV + i) * LANES:(c * PAIRS_PER_KV + i + 1) * LANES]
                 for i in range(PAIRS_PER_KV)], axis=0)
            ox = []
            for p in range(2):
                v_half = low0 if p == 0 else jnp.logical_not(low0)
                kx = clear_slot0(kext_ref[2 * c + p, r0:r0 + 2 * BLOCK, :], slot0)
                vx = clear_slot0(vext_ref[2 * c + p, r0:r0 + 2 * BLOCK, :], jnp.logical_and(slot0, v_half))
                s = lax.dot_general(qs, kx, (((1,), (1,)), ((), ())), preferred_element_type=F32)
                s = s + bias_ref[variant, c, :, p * 2 * BLOCK:(p + 1) * 2 * BLOCK]
                pes = []
                for i in range(PAIRS_PER_KV):
                    si = s[i * BLOCK:(i + 1) * BLOCK]
                    m = jnp.max(si, axis=-1, keepdims=True)
                    pes.append(jnp.exp(si - m).astype(BF16))
                ox.append(_dot(jnp.concatenate(pes, axis=0), vx))
            for i in range(PAIRS_PER_KV):
                pair = c * PAIRS_PER_KV + i
                ox_e = ox[0][i * BLOCK:(i + 1) * BLOCK]
                ox_o = ox[1][i * BLOCK:(i + 1) * BLOCK]
                num = jnp.where(low_blk, ox_e, ox_o)
                den = pltpu.roll(jnp.where(low_blk, ox_o, ox_e), HEAD_DIM, 1)
                oacc_ref[r0:r0 + BLOCK, pair * LANES:(pair + 1) * LANES] = (num / den).astype(BF16)

    kext_ref[:, 0:BLOCK, :] = kext_ref[:, ts:ts + BLOCK, :]
    vext_ref[:, 0:BLOCK, :] = vext_ref[:, ts:ts + BLOCK, :]

    o_ref[...] = x + _dot(oacc_ref[...], wo_ref[...]) + bo_ref[...]


def _attn_mixer(x, g, w_qkv, b_qkv, q_gain, k_gain, sinks, w_o, b_o, rel_bias):
    bsz, s, _ = x.shape
    ts = MIX_TOKENS
    assert s % ts == 0 and ts % BLOCK == 0
    qkv_w = (N_HEADS + 2 * N_KV_HEADS) * HEAD_DIM
    kv_w = N_KV_HEADS * HEAD_DIM
    buckets, in_band = _band_tables()
    qg = jnp.tile(q_gain, N_HEADS).reshape(1, D_MODEL)
    kg = jnp.tile(k_gain, N_KV_HEADS).reshape(1, kv_w)
    smem = pl.BlockSpec(memory_space=pltpu.SMEM)
    return pl.pallas_call(
        _attn_body,
        grid=(bsz, s // ts),
        in_specs=[
            smem,
            smem,
            pl.BlockSpec((None, ts, D_MODEL), lambda b, i: (b, i, 0)),
            _const_spec((1, D_MODEL)),
            _const_spec((D_MODEL, qkv_w)),
            _const_spec((1, qkv_w)),
            _const_spec((1, D_MODEL)),
            _const_spec((1, kv_w)),
            _const_spec((BLOCK, 2 * BLOCK)),
            _const_spec((BLOCK, 2 * BLOCK)),
            _const_spec((D_MODEL, D_MODEL)),
            _const_spec((1, D_MODEL)),
        ],
        out_specs=pl.BlockSpec((None, ts, D_MODEL), lambda b, i: (b, i, 0)),
        out_shape=jax.ShapeDtypeStruct(x.shape, F32),
        scratch_shapes=[
            pltpu.VMEM((2, N_KV_HEADS, PAIRS_PER_KV * BLOCK, 4 * BLOCK), F32),
            pltpu.VMEM((ts, D_MODEL), BF16),
            pltpu.VMEM((2 * N_KV_HEADS, ts + BLOCK, kv_w), BF16),
            pltpu.VMEM((2 * N_KV_HEADS, ts + BLOCK, kv_w), BF16),
            pltpu.VMEM((ts, D_MODEL), BF16),
        ],
        compiler_params=_compiler_params(2),
        name="attn_mixer",
    )(sinks, rel_bias, x, g.reshape(1, D_MODEL), w_qkv.astype(BF16), b_qkv.reshape(1, qkv_w), qg, kg,
      jnp.asarray(buckets), jnp.asarray(in_band), w_o.astype(BF16), b_o.reshape(1, D_MODEL))


def kernel(x, ffn1_norm, ffn1_w_in, ffn1_w_out, mix_norm, ffn2_norm, ffn2_w_in, ffn2_w_out, conv_w_in, conv_w,
           conv_w_out, pool_w, pool_b, pool_scale, attn_w_qkv, attn_b_qkv, attn_q_norm, attn_k_norm, attn_sinks,
           attn_w_o, attn_b_o, rel_bias):
    bsz, s, d = x.shape
    ffn1 = (ffn1_norm.reshape(DEPTH, 1, d), ffn1_w_in, ffn1_w_out)
    ffn2 = (ffn2_norm.reshape(DEPTH, 1, d), ffn2_w_in, ffn2_w_out)

    def ffn(xx, params, layer):
        return _ffn(xx.reshape(bsz * s, d), *params, layer).reshape(bsz, s, d)

    for i in range(DEPTH):
        x = ffn(x, ffn1, i)
        kind, j = i % N_MIXERS, i // N_MIXERS
        if kind == 0:
            x = _conv_mixer(x, mix_norm[i], conv_w_in, conv_w, conv_w_out, j)
        elif kind == 1:
            x = _pool_mixer(x, mix_norm[i], pool_w[j], pool_b[j], pool_scale[j])
        else:
            x = _attn_mixer(x, mix_norm[i], attn_w_qkv[j], attn_b_qkv[j], attn_q_norm[j], attn_k_norm[j],
                            attn_sinks[j], attn_w_o[j], attn_b_o[j], rel_bias)
        x = ffn(x, ffn2, i)
    return x
```

```python
import functools
import math

import numpy as np
import jax
import jax.numpy as jnp
from jax import lax
from jax.experimental import pallas as pl
from jax.experimental.pallas import tpu as pltpu

D_MODEL = 1024
D_FF = 2816
DEPTH = 4
N_MIXERS = 3
CONV_WIDTH = 3
POOL_WINDOWS = (2, 4, 8, 16)
POOL_CH = D_MODEL // len(POOL_WINDOWS)
HEAD_DIM = 64
N_HEADS = D_MODEL // HEAD_DIM
N_KV_HEADS = 2
GQA_GROUP = N_HEADS // N_KV_HEADS
WINDOW = 128
BLOCK = 128
NUM_BUCKETS = 32
MAX_EXACT = NUM_BUCKETS // 2
MAX_DISTANCE = 128
EPS = 1e-6

LANES = 128
SUBLANES = 8
MASK_VALUE = -1e30
VMEM_LIMIT_BYTES = 56 * 1024 * 1024

FFN_TOKENS = 1024
FFN_CHUNK = 256
FFN_SUB = FFN_TOKENS // 2
FFN_PIECE = 64
FETCH_SLOTS = 4
FFN_WIN_ROWS = 64
FFN_WOUT_ROWS = 128
MIX_TOKENS = 1024
CONV_TOKENS = 1024
CONV_SUB = 512
CONV_W_ROWS = 128
POOL_HALO = 16
HEAD_PAIRS = N_HEADS // 2
PAIRS_PER_KV = HEAD_PAIRS // N_KV_HEADS

F32 = jnp.float32
BF16 = jnp.bfloat16


def _rmsnorm_bf16(x, g):
    ms = jnp.mean(x * x, axis=-1, keepdims=True)
    return (x * lax.rsqrt(ms + EPS) * g).astype(BF16)


def _dot(a, b):
    return jnp.dot(a, b, preferred_element_type=F32)


def _const_spec(shape):
    return pl.BlockSpec(shape, lambda *_: (0,) * len(shape), pipeline_mode=pl.Buffered(1))


def _layer_spec(shape, layer):
    return pl.BlockSpec((None,) + shape, lambda *_: (layer,) + (0,) * len(shape), pipeline_mode=pl.Buffered(1))


def _fetch_weights_bf16(streams):
    def copy(stream, k):
        src, _, stage_ref, sem = stream
        slots, rows = stage_ref.shape[0], stage_ref.shape[1]
        return pltpu.make_async_copy(src.at[pl.ds(k * rows, rows)], stage_ref.at[k % slots], sem.at[k % slots])

    counts = []
    for stream in streams:
        src, _, stage_ref, _ = stream
        slots, rows = stage_ref.shape[0], stage_ref.shape[1]
        assert src.shape[0] % rows == 0
        counts.append(src.shape[0] // rows)
        for k in range(min(slots, counts[-1])):
            copy(stream, k).start()
    for k in range(max(counts)):
        for stream, n in zip(streams, counts):
            if k < n:
                _, dst_ref, stage_ref, _ = stream
                slots, rows = stage_ref.shape[0], stage_ref.shape[1]
                copy(stream, k).wait()
                dst_ref[k * rows:(k + 1) * rows, :] = stage_ref[k % slots].astype(BF16)
                if k + slots < n:
                    copy(stream, k + slots).start()


def _compiler_params(n_grid_dims):
    return pltpu.CompilerParams(
        dimension_semantics=("arbitrary",) * n_grid_dims,
        vmem_limit_bytes=VMEM_LIMIT_BYTES,
    )


def _ffn_body(x0_ref, xs_ref, g_ref, win_hbm, wout_hbm, o_ref, h_ref, xres_ref, win_ref, wout_ref, stage_in, stage_out,
              sem_in, sem_out, *, layer):
    i = pl.program_id(0)
    g = g_ref[...]
    own = pl.multiple_of(jnp.where(i == pl.num_programs(0) - 1, FFN_SUB, 0), FFN_SUB)

    @pl.when(i == 0)
    def _():
        _fetch_weights_bf16([(win_hbm.at[layer], win_ref, stage_in, sem_in),
                             (wout_hbm.at[layer], wout_ref, stage_out, sem_out)])
        h_ref[0] = _rmsnorm_bf16(x0_ref[...], g)
        xres_ref[...] = x0_ref[...]

    def norm_piece(src_row, slot, k):
        r = k * FFN_PIECE
        hk = _rmsnorm_bf16(xs_ref[pl.ds(src_row + r, FFN_PIECE), :], g)
        h_ref[slot, r:r + FFN_PIECE, :] = hk
        u = pltpu.bitcast(hk, jnp.uint32)
        t = u[0:SUBLANES]
        for rr in range(SUBLANES, u.shape[0], SUBLANES):
            t = t | u[rr:rr + SUBLANES]
        z = t[:, 0:LANES]
        for cc in range(LANES, D_MODEL, LANES):
            z = z | t[:, cc:cc + LANES]
        return (z >> 16) >> 16

    def attach(a, z):
        head = pltpu.bitcast(pltpu.bitcast(a[0:SUBLANES, 0:LANES], jnp.uint32) | z, F32)
        top = jnp.concatenate([head, a[0:SUBLANES, LANES:]], axis=1)
        return jnp.concatenate([top, a[SUBLANES:]], axis=0)

    for half in range(2):
        slot, other = half, 1 - half
        src_row = own if half == 0 else FFN_SUB
        acc = None
        for c in range(D_FF // FFN_CHUNK):
            lo = c * FFN_CHUNK
            gate = _dot(h_ref[slot], win_ref[:, lo:lo + FFN_CHUNK])
            up = _dot(h_ref[slot], win_ref[:, D_FF + lo:D_FF + lo + FFN_CHUNK])
            if 1 <= c <= FFN_SUB // FFN_PIECE:
                up = attach(up, norm_piece(src_row, other, c - 1))
            act = (gate * jax.nn.sigmoid(gate) * up).astype(BF16)
            part = _dot(act, wout_ref[lo:lo + FFN_CHUNK, :])
            acc = part if acc is None else acc + part
        res = xres_ref[...] if half == 0 else xs_ref[pl.ds(own, FFN_SUB), :]
        o_ref[half * FFN_SUB:(half + 1) * FFN_SUB, :] = res + 0.5 * acc
    xres_ref[...] = xs_ref[FFN_SUB:, :]


def _ffn(x2, g, w_in, w_out, layer):
    t = x2.shape[0]
    assert t % FFN_TOKENS == 0
    return pl.pallas_call(
        functools.partial(_ffn_body, layer=layer),
        grid=(t // FFN_TOKENS,),
        in_specs=[
            _const_spec((FFN_SUB, D_MODEL)),
            pl.BlockSpec((pl.Element(FFN_TOKENS), pl.Element(D_MODEL)),
                         lambda i: (pl.multiple_of(jnp.minimum(i * FFN_TOKENS + FFN_SUB, t - FFN_TOKENS), FFN_SUB), 0)),
            _layer_spec((1, D_MODEL), layer),
            pl.BlockSpec(memory_space=pl.ANY),
            pl.BlockSpec(memory_space=pl.ANY),
        ],
        out_specs=pl.BlockSpec((FFN_TOKENS, D_MODEL), lambda i: (i, 0)),
        out_shape=jax.ShapeDtypeStruct((t, D_MODEL), F32),
        scratch_shapes=[
            pltpu.VMEM((2, FFN_SUB, D_MODEL), BF16),
            pltpu.VMEM((FFN_SUB, D_MODEL), F32),
            pltpu.VMEM((D_MODEL, 2 * D_FF), BF16),
            pltpu.VMEM((D_FF, D_MODEL), BF16),
            pltpu.VMEM((FETCH_SLOTS, FFN_WIN_ROWS, 2 * D_FF), F32),
            pltpu.VMEM((FETCH_SLOTS, FFN_WOUT_ROWS, D_MODEL), F32),
            pltpu.SemaphoreType.DMA((FETCH_SLOTS,)),
            pltpu.SemaphoreType.DMA((FETCH_SLOTS,)),
        ],
        compiler_params=_compiler_params(1),
        name="ffn",
    )(x2, x2, g, w_in, w_out)


def _conv_body(x_ref, g_ref, win_hbm, cw_ref, wout_hbm, o_ref, prev_ref, win_ref, wout_ref, stage_in, stage_out,
               sem_in, sem_out, *, layer):
    ts = x_ref.shape[0]

    @pl.when(jnp.logical_and(pl.program_id(0) == 0, pl.program_id(1) == 0))
    def _():
        _fetch_weights_bf16([(win_hbm.at[layer], win_ref, stage_in, sem_in),
                             (wout_hbm.at[layer], wout_ref, stage_out, sem_out)])

    @pl.when(pl.program_id(1) == 0)
    def _():
        prev_ref[...] = jnp.zeros_like(prev_ref)

    w0 = cw_ref[0:1, :]
    w1 = cw_ref[1:2, :]
    w2 = cw_ref[2:3, :]
    row = lax.broadcasted_iota(jnp.int32, (SUBLANES, D_MODEL), 0)
    prev = prev_ref[...]
    for r in range(0, ts, CONV_SUB):
        x = x_ref[r:r + CONV_SUB, :]
        h = _rmsnorm_bf16(x, g_ref[...])
        c = _dot(h, win_ref[:, D_MODEL:2 * D_MODEL])
        v = _dot(h, win_ref[:, 2 * D_MODEL:3 * D_MODEL])
        z = c * v
        z1 = pltpu.roll(z, 1, 0)
        z2 = pltpu.roll(z, 2, 0)
        head1 = jnp.where(row < 1, pltpu.roll(prev, 1, 0), z1[:SUBLANES])
        head2 = jnp.where(row < 2, pltpu.roll(prev, 2, 0), z2[:SUBLANES])
        conv_head = w0 * head2 + w1 * head1 + w2 * z[:SUBLANES]
        conv_rest = w0 * z2[SUBLANES:] + w1 * z1[SUBLANES:] + w2 * z[SUBLANES:]
        conv = jnp.concatenate([conv_head, conv_rest], axis=0)
        prev = z[CONV_SUB - SUBLANES:, :]

        b = _dot(h, win_ref[:, 0:D_MODEL])
        u = (b * conv).astype(BF16)
        o_ref[r:r + CONV_SUB, :] = x + _dot(u, wout_ref[...])
    prev_ref[...] = prev


def _conv_mixer(x, g, w_in, conv_w, w_out, layer):
    bsz, s, _ = x.shape
    ts = CONV_TOKENS
    assert s % ts == 0
    return pl.pallas_call(
        functools.partial(_conv_body, layer=layer),
        grid=(bsz, s // ts),
        in_specs=[
            pl.BlockSpec((None, ts, D_MODEL), lambda b, i: (b, i, 0)),
            _const_spec((1, D_MODEL)),
            pl.BlockSpec(memory_space=pl.ANY),
            _layer_spec((CONV_WIDTH, D_MODEL), layer),
            pl.BlockSpec(memory_space=pl.ANY),
        ],
        out_specs=pl.BlockSpec((None, ts, D_MODEL), lambda b, i: (b, i, 0)),
        out_shape=jax.ShapeDtypeStruct(x.shape, F32),
        scratch_shapes=[
            pltpu.VMEM((SUBLANES, D_MODEL), F32),
            pltpu.VMEM((D_MODEL, 3 * D_MODEL), BF16),
            pltpu.VMEM((D_MODEL, D_MODEL), BF16),
            pltpu.VMEM((FETCH_SLOTS, CONV_W_ROWS, 3 * D_MODEL), F32),
            pltpu.VMEM((FETCH_SLOTS, CONV_W_ROWS, D_MODEL), F32),
            pltpu.SemaphoreType.DMA((FETCH_SLOTS,)),
            pltpu.SemaphoreType.DMA((FETCH_SLOTS,)),
        ],
        compiler_params=_compiler_params(2),
        name="conv_mixer",
    )(x, g.reshape(1, D_MODEL), w_in, conv_w, w_out)


def _pool_body(x_ref, g_ref, w_ref, b_ref, sc_ref, o_ref, prev_ref):
    ts = x_ref.shape[0]
    i = pl.program_id(1)

    @pl.when(i == 0)
    def _():
        prev_ref[...] = jnp.zeros_like(prev_ref)

    x = x_ref[...]
    ms = jnp.mean(x * x, axis=-1, keepdims=True)
    h = x * lax.rsqrt(ms + EPS) * g_ref[...]
    prev = prev_ref[...]
    prev_ref[...] = h[ts - POOL_HALO:, :]

    pos = i * ts + lax.broadcasted_iota(jnp.int32, (ts, POOL_CH), 0)
    ys = []
    for gi, win in enumerate(POOL_WINDOWS):
        lo = gi * POOL_CH
        hg = h[:, lo:lo + POOL_CH]
        ssum = jnp.concatenate([prev[:, lo:lo + POOL_CH], hg], axis=0)
        shift = 1
        while shift < win:
            ssum = ssum + pltpu.roll(ssum, shift, 0)
            shift *= 2
        cnt = jnp.minimum(pos + 1, win).astype(F32)
        diff = ssum[POOL_HALO:] / cnt - hg
        ys.append(_dot(diff.astype(BF16), w_ref[gi]))
    y = jnp.concatenate(ys, axis=1)
    o_ref[...] = x + (y + b_ref[...]) * sc_ref[...]


def _pool_mixer(x, g, w, bias, scale):
    bsz, s, _ = x.shape
    ts = CONV_TOKENS
    assert s % ts == 0
    ng = len(POOL_WINDOWS)
    return pl.pallas_call(
        _pool_body,
        grid=(bsz, s // ts),
        in_specs=[
            pl.BlockSpec((None, ts, D_MODEL), lambda b, i: (b, i, 0)),
            _const_spec((1, D_MODEL)),
            _const_spec((ng, POOL_CH, POOL_CH)),
            _const_spec((1, D_MODEL)),
            _const_spec((1, D_MODEL)),
        ],
        out_specs=pl.BlockSpec((None, ts, D_MODEL), lambda b, i: (b, i, 0)),
        out_shape=jax.ShapeDtypeStruct(x.shape, F32),
        scratch_shapes=[pltpu.VMEM((POOL_HALO, D_MODEL), F32)],
        compiler_params=_compiler_params(2),
        name="pool_mixer",
    )(x, g.reshape(1, D_MODEL), w.astype(BF16), bias.reshape(1, D_MODEL), scale.reshape(1, D_MODEL))


def _t5_causal_buckets(n):
    nf = np.maximum(n, 1).astype(np.float32)
    large = MAX_EXACT + (np.log(nf / MAX_EXACT) / math.log(MAX_DISTANCE / MAX_EXACT)
                         * (NUM_BUCKETS - MAX_EXACT)).astype(np.int32)
    large = np.minimum(large, NUM_BUCKETS - 1)
    return np.where(n < MAX_EXACT, n, large).astype(np.int32)


def _band_tables():
    qi = np.arange(BLOCK)[:, None]
    ki = np.arange(2 * BLOCK)[None, :]
    dist = qi + BLOCK - ki
    in_band = ((dist >= 0) & (dist < WINDOW)).astype(np.int32)
    assert not in_band[:, 0].any()
    return _t5_causal_buckets(dist), in_band


def _group_sumsq(x, ones_blk):
    sq = x * x
    hi = sq.astype(BF16)
    lo = (sq - hi.astype(F32)).astype(BF16)
    n = x.shape[1]
    w = ones_blk.shape[0]
    outs = [_dot(hi[:, s:s + w], ones_blk) + _dot(lo[:, s:s + w], ones_blk) for s in range(0, n, w)]
    return outs[0] if len(outs) == 1 else jnp.concatenate(outs, axis=1)


def _attn_body(sink_ref, relb_ref, x_ref, g_ref, wqkv_ref, bqkv_ref, qg_ref, kg_ref, bkt_ref, band_ref,
               wo_ref, bo_ref, o_ref, bias_ref, q_ref, kext_ref, vext_ref, oacc_ref):
    ts = x_ref.shape[0]
    nblk = ts // BLOCK
    first_tile = pl.program_id(1) == 0
    kv_w = N_KV_HEADS * HEAD_DIM

    @pl.when(jnp.logical_and(pl.program_id(0) == 0, pl.program_id(1) == 0))
    def _():
        bkt = bkt_ref[...]
        band = band_ref[...] > 0
        key = lax.broadcasted_iota(jnp.int32, (BLOCK, 2 * BLOCK), 1)
        has_prev = key >= BLOCK
        sink_slot = key == 0
        for kv in range(N_KV_HEADS):
            for pair in range(PAIRS_PER_KV):
                for par in range(2):
                    head = kv * GQA_GROUP + 2 * pair + par

                    def pick(b, acc):
                        return jnp.where(bkt == b, relb_ref[b, head], acc)

                    vals = lax.fori_loop(0, NUM_BUCKETS, pick, jnp.zeros((BLOCK, 2 * BLOCK), F32))
                    vals = jnp.where(band, vals, MASK_VALUE)
                    sink = sink_ref[head]
                    rows = slice(pair * BLOCK, (pair + 1) * BLOCK)
                    cols = slice(par * 2 * BLOCK, (par + 1) * 2 * BLOCK)
                    bias_ref[0, kv, rows, cols] = jnp.where(sink_slot, sink, vals)
                    bias_ref[1, kv, rows, cols] = jnp.where(sink_slot, sink, jnp.where(has_prev, vals, MASK_VALUE))

    @pl.when(first_tile)
    def _():
        kext_ref[:, 0:BLOCK, :] = jnp.zeros((2 * N_KV_HEADS, BLOCK, kv_w), BF16)
        low_prev = lax.broadcasted_iota(jnp.int32, (BLOCK, kv_w), 1) < HEAD_DIM
        for c in range(N_KV_HEADS):
            vext_ref[2 * c, 0:BLOCK, :] = jnp.where(low_prev, 0.0, 1.0).astype(BF16)
            vext_ref[2 * c + 1, 0:BLOCK, :] = jnp.where(low_prev, 1.0, 0.0).astype(BF16)

    x = x_ref[...]
    h = _rmsnorm_bf16(x, g_ref[...])

    q = _dot(h, wqkv_ref[:, 0:D_MODEL]) + bqkv_ref[:, 0:D_MODEL]
    head_shift = HEAD_DIM.bit_length() - 1
    r4 = lax.broadcasted_iota(jnp.int32, (2 * LANES, 2 * LANES), 0) >> head_shift
    c4 = lax.broadcasted_iota(jnp.int32, (2 * LANES, 2 * LANES), 1) >> head_shift
    ones4 = jnp.where(r4 == c4, 1.0, 0.0).astype(BF16)
    qss = _group_sumsq(q, ones4)
    q_ref[...] = (q * lax.rsqrt(qss + HEAD_DIM * EPS) * qg_ref[...]).astype(BF16)

    kvp = _dot(h, wqkv_ref[:, D_MODEL:D_MODEL + 2 * kv_w]) + bqkv_ref[:, D_MODEL:D_MODEL + 2 * kv_w]
    k = kvp[:, 0:kv_w]
    v = kvp[:, kv_w:2 * kv_w]
    ones2 = ones4[0:kv_w, 0:kv_w]
    kss = _group_sumsq(k, ones2)
    kn = k * lax.rsqrt(kss * (1.0 / HEAD_DIM) + EPS) * kg_ref[...]

    lane = lax.broadcasted_iota(jnp.int32, (ts, kv_w), 1)
    low = lane < HEAD_DIM
    kn_sw = pltpu.roll(kn, HEAD_DIM, 1)
    v_sw = pltpu.roll(v, HEAD_DIM, 1)
    for c in range(N_KV_HEADS):
        for p in range(2):
            half = low if p == 0 else jnp.logical_not(low)
            ksrc = kn if c == p else kn_sw
            vsrc = v if c == p else v_sw
            kext_ref[2 * c + p, BLOCK:, :] = jnp.where(half, ksrc, 0.0).astype(BF16)
            vext_ref[2 * c + p, BLOCK:, :] = jnp.where(half, vsrc, 1.0).astype(BF16)

    low_blk = lax.broadcasted_iota(jnp.int32, (BLOCK, LANES), 1) < HEAD_DIM
    bf16_rows = 2 * SUBLANES
    slot0 = lax.broadcasted_iota(jnp.int32, (bf16_rows, LANES), 0) == 0
    low0 = lax.broadcasted_iota(jnp.int32, (bf16_rows, LANES), 1) < HEAD_DIM
    zeros0 = jnp.zeros((bf16_rows, LANES), BF16)

    def clear_slot0(a, mask):
        return jnp.concatenate([jnp.where(mask, zeros0, a[:bf16_rows]), a[bf16_rows:]], axis=0)

    for j in range(nblk):
        r0 = j * BLOCK
        variant = jnp.where(first_tile, 1, 0) if j == 0 else 0
        for c in range(N_KV_HEADS):
            qs = jnp.concatenate(
                [q_ref[r0:r0 + BLOCK, (c * PAIRS_PER_KV + i) * LANES:(c * PAIRS_PER_KV + i + 1) * LANES]
                 for i in range(PAIRS_PER_KV)], axis=0)
            ox = []
            for p in range(2):
                v_half = low0 if p == 0 else jnp.logical_not(low0)
                kx = clear_slot0(kext_ref[2 * c + p, r0:r0 + 2 * BLOCK, :], slot0)
                vx = clear_slot0(vext_ref[2 * c + p, r0:r0 + 2 * BLOCK, :], jnp.logical_and(slot0, v_half))
                s = lax.dot_general(qs, kx, (((1,), (1,)), ((), ())), preferred_element_type=F32)
                s = s + bias_ref[variant, c, :, p * 2 * BLOCK:(p + 1) * 2 * BLOCK]
                pes = []
                for i in range(PAIRS_PER_KV):
                    si = s[i * BLOCK:(i + 1) * BLOCK]
                    m = jnp.max(si, axis=-1, keepdims=True)
                    pes.append(jnp.exp(si - m).astype(BF16))
                ox.append(_dot(jnp.concatenate(pes, axis=0), vx))
            for i in range(PAIRS_PER_KV):
                pair = c * PAIRS_PER_KV + i
                ox_e = ox[0][i * BLOCK:(i + 1) * BLOCK]
                ox_o = ox[1][i * BLOCK:(i + 1) * BLOCK]
                num = jnp.where(low_blk, ox_e, ox_o)
                den = pltpu.roll(jnp.where(low_blk, ox_o, ox_e), HEAD_DIM, 1)
                oacc_ref[r0:r0 + BLOCK, pair * LANES:(pair + 1) * LANES] = (num / den).astype(BF16)

    kext_ref[:, 0:BLOCK, :] = kext_ref[:, ts:ts + BLOCK, :]
    vext_ref[:, 0:BLOCK, :] = vext_ref[:, ts:ts + BLOCK, :]

    o_ref[...] = x + _dot(oacc_ref[...], wo_ref[...]) + bo_ref[...]


def _attn_mixer(x, g, w_qkv, b_qkv, q_gain, k_gain, sinks, w_o, b_o, rel_bias):
    bsz, s, _ = x.shape
    ts = MIX_TOKENS
    assert s % ts == 0 and ts % BLOCK == 0
    qkv_w = (N_HEADS + 2 * N_KV_HEADS) * HEAD_DIM
    kv_w = N_KV_HEADS * HEAD_DIM
    buckets, in_band = _band_tables()
    qg = jnp.tile(q_gain, N_HEADS).reshape(1, D_MODEL)
    kg = jnp.tile(k_gain, N_KV_HEADS).reshape(1, kv_w)
    smem = pl.BlockSpec(memory_space=pltpu.SMEM)
    return pl.pallas_call(
        _attn_body,
        grid=(bsz, s // ts),
        in_specs=[
            smem,
            smem,
            pl.BlockSpec((None, ts, D_MODEL), lambda b, i: (b, i, 0)),
            _const_spec((1, D_MODEL)),
            _const_spec((D_MODEL, qkv_w)),
            _const_spec((1, qkv_w)),
            _const_spec((1, D_MODEL)),
            _const_spec((1, kv_w)),
            _const_spec((BLOCK, 2 * BLOCK)),
            _const_spec((BLOCK, 2 * BLOCK)),
            _const_spec((D_MODEL, D_MODEL)),
            _const_spec((1, D_MODEL)),
        ],
        out_specs=pl.BlockSpec((None, ts, D_MODEL), lambda b, i: (b, i, 0)),
        out_shape=jax.ShapeDtypeStruct(x.shape, F32),
        scratch_shapes=[
            pltpu.VMEM((2, N_KV_HEADS, PAIRS_PER_KV * BLOCK, 4 * BLOCK), F32),
            pltpu.VMEM((ts, D_MODEL), BF16),
            pltpu.VMEM((2 * N_KV_HEADS, ts + BLOCK, kv_w), BF16),
            pltpu.VMEM((2 * N_KV_HEADS, ts + BLOCK, kv_w), BF16),
            pltpu.VMEM((ts, D_MODEL), BF16),
        ],
        compiler_params=_compiler_params(2),
        name="attn_mixer",
    )(sinks, rel_bias, x, g.reshape(1, D_MODEL), w_qkv.astype(BF16), b_qkv.reshape(1, qkv_w), qg, kg,
      jnp.asarray(buckets), jnp.asarray(in_band), w_o.astype(BF16), b_o.reshape(1, D_MODEL))


def kernel(x, ffn1_norm, ffn1_w_in, ffn1_w_out, mix_norm, ffn2_norm, ffn2_w_in, ffn2_w_out, conv_w_in, conv_w,
           conv_w_out, pool_w, pool_b, pool_scale, attn_w_qkv, attn_b_qkv, attn_q_norm, attn_k_norm, attn_sinks,
           attn_w_o, attn_b_o, rel_bias):
    bsz, s, d = x.shape
    ffn1 = (ffn1_norm.reshape(DEPTH, 1, d), ffn1_w_in, ffn1_w_out)
    ffn2 = (ffn2_norm.reshape(DEPTH, 1, d), ffn2_w_in, ffn2_w_out)

    def ffn(xx, params, layer):
        return _ffn(xx.reshape(bsz * s, d), *params, layer).reshape(bsz, s, d)

    for i in range(DEPTH):
        x = ffn(x, ffn1, i)
        kind, j = i % N_MIXERS, i // N_MIXERS
        if kind == 0:
            x = _conv_mixer(x, mix_norm[i], conv_w_in, conv_w, conv_w_out, j)
        elif kind == 1:
            x = _pool_mixer(x, mix_norm[i], pool_w[j], pool_b[j], pool_scale[j])
        else:
            x = _attn_mixer(x, mix_norm[i], attn_w_qkv[j], attn_b_qkv[j], attn_q_norm[j], attn_k_norm[j],
                            attn_sinks[j], attn_w_o[j], attn_b_o[j], rel_bias)
        x = ffn(x, ffn2, i)
    return x
```

```python
import functools
import math

import numpy as np
import jax
import jax.numpy as jnp
from jax import lax
from jax.experimental import pallas as pl
from jax.experimental.pallas import tpu as pltpu

D_MODEL = 1024
D_FF = 2816
DEPTH = 4
N_MIXERS = 3
CONV_WIDTH = 3
POOL_WINDOWS = (2, 4, 8, 16)
POOL_CH = D_MODEL // len(POOL_WINDOWS)
HEAD_DIM = 64
N_HEADS = D_MODEL // HEAD_DIM
N_KV_HEADS = 2
GQA_GROUP = N_HEADS // N_KV_HEADS
WINDOW = 128
BLOCK = 128
NUM_BUCKETS = 32
MAX_EXACT = NUM_BUCKETS // 2
MAX_DISTANCE = 128
EPS = 1e-6

LANES = 128
SUBLANES = 8
MASK_VALUE = -1e30
VMEM_LIMIT_BYTES = 56 * 1024 * 1024

FFN_TOKENS = 1024
FFN_CHUNK = 256
FFN_SUB = FFN_TOKENS // 2
FFN_PIECE = 64
FETCH_SLOTS = 4
FFN_WIN_ROWS = 64
FFN_WOUT_ROWS = 128
MIX_TOKENS = 1024
Q_BLOCKS = 4
CONV_TOKENS = 1024
CONV_SUB = 512
CONV_W_ROWS = 128
POOL_HALO = 16
HEAD_PAIRS = N_HEADS // 2
PAIRS_PER_KV = HEAD_PAIRS // N_KV_HEADS

F32 = jnp.float32
BF16 = jnp.bfloat16


def _rmsnorm_bf16(x, g):
    ms = jnp.mean(x * x, axis=-1, keepdims=True)
    return (x * lax.rsqrt(ms + EPS) * g).astype(BF16)


def _dot(a, b):
    return jnp.dot(a, b, preferred_element_type=F32)


def _const_spec(shape):
    return pl.BlockSpec(shape, lambda *_: (0,) * len(shape), pipeline_mode=pl.Buffered(1))


def _layer_spec(shape, layer):
    return pl.BlockSpec((None,) + shape, lambda *_: (layer,) + (0,) * len(shape), pipeline_mode=pl.Buffered(1))


def _fetch_weights_bf16(streams):
    def copy(stream, k):
        src, _, stage_ref, sem = stream
        slots, rows = stage_ref.shape[0], stage_ref.shape[1]
        return pltpu.make_async_copy(src.at[pl.ds(k * rows, rows)], stage_ref.at[k % slots], sem.at[k % slots])

    counts = []
    for stream in streams:
        src, _, stage_ref, _ = stream
        slots, rows = stage_ref.shape[0], stage_ref.shape[1]
        assert src.shape[0] % rows == 0
        counts.append(src.shape[0] // rows)
        for k in range(min(slots, counts[-1])):
            copy(stream, k).start()
    for k in range(max(counts)):
        for stream, n in zip(streams, counts):
            if k < n:
                _, dst_ref, stage_ref, _ = stream
                slots, rows = stage_ref.shape[0], stage_ref.shape[1]
                copy(stream, k).wait()
                dst_ref[k * rows:(k + 1) * rows, :] = stage_ref[k % slots].astype(BF16)
                if k + slots < n:
                    copy(stream, k + slots).start()


def _compiler_params(n_grid_dims):
    return pltpu.CompilerParams(
        dimension_semantics=("arbitrary",) * n_grid_dims,
        vmem_limit_bytes=VMEM_LIMIT_BYTES,
    )


def _ffn_body(x_ref, xn_ref, g_ref, win_hbm, wout_hbm, o_ref, h_ref, win_ref, wout_ref, stage_in, stage_out, sem_in,
              sem_out, *, layer):
    g = g_ref[...]

    @pl.when(pl.program_id(0) == 0)
    def _():
        _fetch_weights_bf16([(win_hbm.at[layer], win_ref, stage_in, sem_in),
                             (wout_hbm.at[layer], wout_ref, stage_out, sem_out)])
        h_ref[0] = _rmsnorm_bf16(x_ref[0:FFN_SUB, :], g)

    def norm_piece(src_ref, src_row, slot, k):
        r = k * FFN_PIECE
        hk = _rmsnorm_bf16(src_ref[src_row + r:src_row + r + FFN_PIECE, :], g)
        h_ref[slot, r:r + FFN_PIECE, :] = hk
        u = pltpu.bitcast(hk, jnp.uint32)
        t = u[0:SUBLANES]
        for rr in range(SUBLANES, u.shape[0], SUBLANES):
            t = t | u[rr:rr + SUBLANES]
        z = t[:, 0:LANES]
        for cc in range(LANES, D_MODEL, LANES):
            z = z | t[:, cc:cc + LANES]
        return (z >> 16) >> 16

    def attach(a, z):
        head = pltpu.bitcast(pltpu.bitcast(a[0:SUBLANES, 0:LANES], jnp.uint32) | z, F32)
        top = jnp.concatenate([head, a[0:SUBLANES, LANES:]], axis=1)
        return jnp.concatenate([top, a[SUBLANES:]], axis=0)

    for half in range(2):
        slot, other = half, 1 - half
        src_ref, src_row = (x_ref, FFN_SUB) if half == 0 else (xn_ref, 0)
        acc = None
        for c in range(D_FF // FFN_CHUNK):
            lo = c * FFN_CHUNK
            gate = _dot(h_ref[slot], win_ref[:, lo:lo + FFN_CHUNK])
            up = _dot(h_ref[slot], win_ref[:, D_FF + lo:D_FF + lo + FFN_CHUNK])
            if 1 <= c <= FFN_SUB // FFN_PIECE:
                up = attach(up, norm_piece(src_ref, src_row, other, c - 1))
            act = (gate * jax.nn.sigmoid(gate) * up).astype(BF16)
            part = _dot(act, wout_ref[lo:lo + FFN_CHUNK, :])
            acc = part if acc is None else acc + part
        rows = slice(half * FFN_SUB, (half + 1) * FFN_SUB)
        o_ref[rows, :] = x_ref[rows, :] + 0.5 * acc


def _ffn(x2, g, w_in, w_out, layer):
    t = x2.shape[0]
    assert t % FFN_TOKENS == 0
    n_sub = t // FFN_SUB
    return pl.pallas_call(
        functools.partial(_ffn_body, layer=layer),
        grid=(t // FFN_TOKENS,),
        in_specs=[
            pl.BlockSpec((FFN_TOKENS, D_MODEL), lambda i: (i, 0)),
            pl.BlockSpec((FFN_SUB, D_MODEL), lambda i: (jnp.minimum(2 * i + 2, n_sub - 1), 0)),
            _layer_spec((1, D_MODEL), layer),
            pl.BlockSpec(memory_space=pl.ANY),
            pl.BlockSpec(memory_space=pl.ANY),
        ],
        out_specs=pl.BlockSpec((FFN_TOKENS, D_MODEL), lambda i: (i, 0)),
        out_shape=jax.ShapeDtypeStruct((t, D_MODEL), F32),
        scratch_shapes=[
            pltpu.VMEM((2, FFN_SUB, D_MODEL), BF16),
            pltpu.VMEM((D_MODEL, 2 * D_FF), BF16),
            pltpu.VMEM((D_FF, D_MODEL), BF16),
            pltpu.VMEM((FETCH_SLOTS, FFN_WIN_ROWS, 2 * D_FF), F32),
            pltpu.VMEM((FETCH_SLOTS, FFN_WOUT_ROWS, D_MODEL), F32),
            pltpu.SemaphoreType.DMA((FETCH_SLOTS,)),
            pltpu.SemaphoreType.DMA((FETCH_SLOTS,)),
        ],
        compiler_params=_compiler_params(1),
        name="ffn",
    )(x2, x2, g, w_in, w_out)


def _conv_body(x_ref, g_ref, win_hbm, cw_ref, wout_hbm, o_ref, prev_ref, win_ref, wout_ref, stage_in, stage_out,
               sem_in, sem_out, *, layer):
    ts = x_ref.shape[0]

    @pl.when(jnp.logical_and(pl.program_id(0) == 0, pl.program_id(1) == 0))
    def _():
        _fetch_weights_bf16([(win_hbm.at[layer], win_ref, stage_in, sem_in),
                             (wout_hbm.at[layer], wout_ref, stage_out, sem_out)])

    @pl.when(pl.program_id(1) == 0)
    def _():
        prev_ref[...] = jnp.zeros_like(prev_ref)

    w0 = cw_ref[0:1, :]
    w1 = cw_ref[1:2, :]
    w2 = cw_ref[2:3, :]
    row = lax.broadcasted_iota(jnp.int32, (SUBLANES, D_MODEL), 0)
    prev = prev_ref[...]
    for r in range(0, ts, CONV_SUB):
        x = x_ref[r:r + CONV_SUB, :]
        h = _rmsnorm_bf16(x, g_ref[...])
        c = _dot(h, win_ref[:, D_MODEL:2 * D_MODEL])
        v = _dot(h, win_ref[:, 2 * D_MODEL:3 * D_MODEL])
        z = c * v
        z1 = pltpu.roll(z, 1, 0)
        z2 = pltpu.roll(z, 2, 0)
        head1 = jnp.where(row < 1, pltpu.roll(prev, 1, 0), z1[:SUBLANES])
        head2 = jnp.where(row < 2, pltpu.roll(prev, 2, 0), z2[:SUBLANES])
        conv_head = w0 * head2 + w1 * head1 + w2 * z[:SUBLANES]
        conv_rest = w0 * z2[SUBLANES:] + w1 * z1[SUBLANES:] + w2 * z[SUBLANES:]
        conv = jnp.concatenate([conv_head, conv_rest], axis=0)
        prev = z[CONV_SUB - SUBLANES:, :]

        b = _dot(h, win_ref[:, 0:D_MODEL])
        u = (b * conv).astype(BF16)
        o_ref[r:r + CONV_SUB, :] = x + _dot(u, wout_ref[...])
    prev_ref[...] = prev


def _conv_mixer(x, g, w_in, conv_w, w_out, layer):
    bsz, s, _ = x.shape
    ts = CONV_TOKENS
    assert s % ts == 0
    return pl.pallas_call(
        functools.partial(_conv_body, layer=layer),
        grid=(bsz, s // ts),
        in_specs=[
            pl.BlockSpec((None, ts, D_MODEL), lambda b, i: (b, i, 0)),
            _const_spec((1, D_MODEL)),
            pl.BlockSpec(memory_space=pl.ANY),
            _layer_spec((CONV_WIDTH, D_MODEL), layer),
            pl.BlockSpec(memory_space=pl.ANY),
        ],
        out_specs=pl.BlockSpec((None, ts, D_MODEL), lambda b, i: (b, i, 0)),
        out_shape=jax.ShapeDtypeStruct(x.shape, F32),
        scratch_shapes=[
            pltpu.VMEM((SUBLANES, D_MODEL), F32),
            pltpu.VMEM((D_MODEL, 3 * D_MODEL), BF16),
            pltpu.VMEM((D_MODEL, D_MODEL), BF16),
            pltpu.VMEM((FETCH_SLOTS, CONV_W_ROWS, 3 * D_MODEL), F32),
            pltpu.VMEM((FETCH_SLOTS, CONV_W_ROWS, D_MODEL), F32),
            pltpu.SemaphoreType.DMA((FETCH_SLOTS,)),
            pltpu.SemaphoreType.DMA((FETCH_SLOTS,)),
        ],
        compiler_params=_compiler_params(2),
        name="conv_mixer",
    )(x, g.reshape(1, D_MODEL), w_in, conv_w, w_out)


def _pool_body(x_ref, g_ref, w_ref, b_ref, sc_ref, o_ref, prev_ref):
    ts = x_ref.shape[0]
    i = pl.program_id(1)

    @pl.when(i == 0)
    def _():
        prev_ref[...] = jnp.zeros_like(prev_ref)

    x = x_ref[...]
    ms = jnp.mean(x * x, axis=-1, keepdims=True)
    h = x * lax.rsqrt(ms + EPS) * g_ref[...]
    prev = prev_ref[...]
    prev_ref[...] = h[ts - POOL_HALO:, :]

    pos = i * ts + lax.broadcasted_iota(jnp.int32, (ts, POOL_CH), 0)
    ys = []
    for gi, win in enumerate(POOL_WINDOWS):
        lo = gi * POOL_CH
        hg = h[:, lo:lo + POOL_CH]
        ssum = jnp.concatenate([prev[:, lo:lo + POOL_CH], hg], axis=0)
        shift = 1
        while shift < win:
            ssum = ssum + pltpu.roll(ssum, shift, 0)
            shift *= 2
        cnt = jnp.minimum(pos + 1, win).astype(F32)
        diff = ssum[POOL_HALO:] / cnt - hg
        ys.append(_dot(diff.astype(BF16), w_ref[gi]))
    y = jnp.concatenate(ys, axis=1)
    o_ref[...] = x + (y + b_ref[...]) * sc_ref[...]


def _pool_mixer(x, g, w, bias, scale):
    bsz, s, _ = x.shape
    ts = CONV_TOKENS
    assert s % ts == 0
    ng = len(POOL_WINDOWS)
    return pl.pallas_call(
        _pool_body,
        grid=(bsz, s // ts),
        in_specs=[
            pl.BlockSpec((None, ts, D_MODEL), lambda b, i: (b, i, 0)),
            _const_spec((1, D_MODEL)),
            _const_spec((ng, POOL_CH, POOL_CH)),
            _const_spec((1, D_MODEL)),
            _const_spec((1, D_MODEL)),
        ],
        out_specs=pl.BlockSpec((None, ts, D_MODEL), lambda b, i: (b, i, 0)),
        out_shape=jax.ShapeDtypeStruct(x.shape, F32),
        scratch_shapes=[pltpu.VMEM((POOL_HALO, D_MODEL), F32)],
        compiler_params=_compiler_params(2),
        name="pool_mixer",
    )(x, g.reshape(1, D_MODEL), w.astype(BF16), bias.reshape(1, D_MODEL), scale.reshape(1, D_MODEL))


def _t5_causal_buckets(n):
    nf = np.maximum(n, 1).astype(np.float32)
    large = MAX_EXACT + (np.log(nf / MAX_EXACT) / math.log(MAX_DISTANCE / MAX_EXACT)
                         * (NUM_BUCKETS - MAX_EXACT)).astype(np.int32)
    large = np.minimum(large, NUM_BUCKETS - 1)
    return np.where(n < MAX_EXACT, n, large).astype(np.int32)


def _band_tables():
    qi = np.arange(BLOCK)[:, None]
    ki = np.arange(2 * BLOCK)[None, :]
    dist = qi + BLOCK - ki
    in_band = ((dist >= 0) & (dist < WINDOW)).astype(np.int32)
    assert not in_band[:, 0].any()
    return _t5_causal_buckets(dist), in_band


def _group_sumsq(x, ones_blk):
    sq = x * x
    hi = sq.astype(BF16)
    lo = (sq - hi.astype(F32)).astype(BF16)
    n = x.shape[1]
    w = ones_blk.shape[0]
    outs = [_dot(hi[:, s:s + w], ones_blk) + _dot(lo[:, s:s + w], ones_blk) for s in range(0, n, w)]
    return outs[0] if len(outs) == 1 else jnp.concatenate(outs, axis=1)


def _attn_body(sink_ref, relb_ref, x_ref, g_ref, wqkv_ref, bqkv_ref, qg_ref, kg_ref, bkt_ref, band_ref,
               wo_ref, bo_ref, o_ref, bias_ref, q_ref, kext_ref, vext_ref, oacc_ref):
    ts = x_ref.shape[0]
    nblk = ts // BLOCK
    first_tile = pl.program_id(1) == 0
    kv_w = N_KV_HEADS * HEAD_DIM

    @pl.when(jnp.logical_and(pl.program_id(0) == 0, pl.program_id(1) == 0))
    def _():
        bkt = bkt_ref[...]
        band = band_ref[...] > 0
        key = lax.broadcasted_iota(jnp.int32, (BLOCK, 2 * BLOCK), 1)
        has_prev = key >= BLOCK
        sink_slot = key == 0
        for kv in range(N_KV_HEADS):
            for pair in range(PAIRS_PER_KV):
                for par in range(2):
                    head = kv * GQA_GROUP + 2 * pair + par

                    def pick(b, acc):
                        return jnp.where(bkt == b, relb_ref[b, head], acc)

                    vals = lax.fori_loop(0, NUM_BUCKETS, pick, jnp.zeros((BLOCK, 2 * BLOCK), F32))
                    vals = jnp.where(band, vals, MASK_VALUE)
                    sink = sink_ref[head]
                    rows = slice(pair * BLOCK, (pair + 1) * BLOCK)
                    cols = slice(par * 2 * BLOCK, (par + 1) * 2 * BLOCK)
                    bias_ref[0, kv, rows, cols] = jnp.where(sink_slot, sink, vals)
                    bias_ref[1, kv, rows, cols] = jnp.where(sink_slot, sink, jnp.where(has_prev, vals, MASK_VALUE))

    @pl.when(first_tile)
    def _():
        kext_ref[:, 0:BLOCK, :] = jnp.zeros((2 * N_KV_HEADS, BLOCK, kv_w), BF16)
        low_prev = lax.broadcasted_iota(jnp.int32, (BLOCK, kv_w), 1) < HEAD_DIM
        for c in range(N_KV_HEADS):
            vext_ref[2 * c, 0:BLOCK, :] = jnp.where(low_prev, 0.0, 1.0).astype(BF16)
            vext_ref[2 * c + 1, 0:BLOCK, :] = jnp.where(low_prev, 1.0, 0.0).astype(BF16)

    x = x_ref[...]
    h = _rmsnorm_bf16(x, g_ref[...])

    head_shift = HEAD_DIM.bit_length() - 1
    r4 = lax.broadcasted_iota(jnp.int32, (2 * LANES, 2 * LANES), 0) >> head_shift
    c4 = lax.broadcasted_iota(jnp.int32, (2 * LANES, 2 * LANES), 1) >> head_shift
    ones4 = jnp.where(r4 == c4, 1.0, 0.0).astype(BF16)

    qrows = ts // Q_BLOCKS

    def q_dot(b):
        return _dot(h[b * qrows:(b + 1) * qrows], wqkv_ref[:, 0:D_MODEL]) + bqkv_ref[:, 0:D_MODEL]

    def q_store(b, q):
        qss = _group_sumsq(q, ones4)
        q_ref[b * qrows:(b + 1) * qrows, :] = (q * lax.rsqrt(qss + HEAD_DIM * EPS) * qg_ref[...]).astype(BF16)

    q0 = q_dot(0)
    kvp = _dot(h, wqkv_ref[:, D_MODEL:D_MODEL + 2 * kv_w]) + bqkv_ref[:, D_MODEL:D_MODEL + 2 * kv_w]
    q1 = q_dot(1)
    k = kvp[:, 0:kv_w]
    v = kvp[:, kv_w:2 * kv_w]
    ones2 = ones4[0:kv_w, 0:kv_w]
    kss = _group_sumsq(k, ones2)
    q_store(0, q0)
    q2 = q_dot(2)
    kn = k * lax.rsqrt(kss * (1.0 / HEAD_DIM) + EPS) * kg_ref[...]

    lane = lax.broadcasted_iota(jnp.int32, (ts, kv_w), 1)
    low = lane < HEAD_DIM
    kn_sw = pltpu.roll(kn, HEAD_DIM, 1)
    v_sw = pltpu.roll(v, HEAD_DIM, 1)
    zkv = None
    for c in range(N_KV_HEADS):
        for p in range(2):
            half = low if p == 0 else jnp.logical_not(low)
            ksrc = kn if c == p else kn_sw
            vsrc = v if c == p else v_sw
            for ref, staged in ((kext_ref, jnp.where(half, ksrc, 0.0).astype(BF16)),
                                (vext_ref, jnp.where(half, vsrc, 1.0).astype(BF16))):
                ref[2 * c + p, BLOCK:, :] = staged
                u = pltpu.bitcast(staged, jnp.uint32)
                t = u[0:SUBLANES]
                for rr in range(SUBLANES, u.shape[0], SUBLANES):
                    t = t | u[rr:rr + SUBLANES]
                zkv = t if zkv is None else zkv | t
    zkv = (zkv >> 16) >> 16
    q_store(1, q1)
    q3 = q_dot(3)
    q_store(2, q2)
    head = pltpu.bitcast(pltpu.bitcast(q3[0:SUBLANES, 0:LANES], jnp.uint32) | zkv, F32)
    q3 = jnp.concatenate([jnp.concatenate([head, q3[0:SUBLANES, LANES:]], axis=1), q3[SUBLANES:]], axis=0)
    q_store(3, q3)

    low_blk = lax.broadcasted_iota(jnp.int32, (BLOCK, LANES), 1) < HEAD_DIM
    bf16_rows = 2 * SUBLANES
    slot0 = lax.broadcasted_iota(jnp.int32, (bf16_rows, LANES), 0) == 0
    low0 = lax.broadcasted_iota(jnp.int32, (bf16_rows, LANES), 1) < HEAD_DIM
    zeros0 = jnp.zeros((bf16_rows, LANES), BF16)

    def clear_slot0(a, mask):
        return jnp.concatenate([jnp.where(mask, zeros0, a[:bf16_rows]), a[bf16_rows:]], axis=0)

    for j in range(nblk):
        r0 = j * BLOCK
        variant = jnp.where(first_tile, 1, 0) if j == 0 else 0
        for c in range(N_KV_HEADS):
            qs = jnp.concatenate(
                [q_ref[r0:r0 + BLOCK, (c * PAIRS_PER_KV + i) * LANES:(c * PAIRS_PER_KV + i + 1) * LANES]
                 for i in range(PAIRS_PER_KV)], axis=0)
            ox = []
            for p in range(2):
                v_half = low0 if p == 0 else jnp.logical_not(low0)
                kx = clear_slot0(kext_ref[2 * c + p, r0:r0 + 2 * BLOCK, :], slot0)
                vx = clear_slot0(vext_ref[2 * c + p, r0:r0 + 2 * BLOCK, :], jnp.logical_and(slot0, v_half))
                s = lax.dot_general(qs, kx, (((1,), (1,)), ((), ())), preferred_element_type=F32)
                s = s + bias_ref[variant, c, :, p * 2 * BLOCK:(p + 1) * 2 * BLOCK]
                pes = []
                for i in range(PAIRS_PER_KV):
                    si = s[i * BLOCK:(i + 1) * BLOCK]
                    m = jnp.max(si, axis=-1, keepdims=True)
                    pes.append(jnp.exp(si - m).astype(BF16))
                ox.append(_dot(jnp.concatenate(pes, axis=0), vx))
            for i in range(PAIRS_PER_KV):
                pair = c * PAIRS_PER_KV + i
                ox_e = ox[0][i * BLOCK:(i + 1) * BLOCK]
                ox_o = ox[1][i * BLOCK:(i + 1) * BLOCK]
                num = jnp.where(low_blk, ox_e, ox_o)
                den = pltpu.roll(jnp.where(low_blk, ox_o, ox_e), HEAD_DIM, 1)
                oacc_ref[r0:r0 + BLOCK, pair * LANES:(pair + 1) * LANES] = (num / den).astype(BF16)

    kext_ref[:, 0:BLOCK, :] = kext_ref[:, ts:ts + BLOCK, :]
    vext_ref[:, 0:BLOCK, :] = vext_ref[:, ts:ts + BLOCK, :]

    o_ref[...] = x + _dot(oacc_ref[...], wo_ref[...]) + bo_ref[...]


def _attn_mixer(x, g, w_qkv, b_qkv, q_gain, k_gain, sinks, w_o, b_o, rel_bias):
    bsz, s, _ = x.shape
    ts = MIX_TOKENS
    assert s % ts == 0 and ts % BLOCK == 0
    qkv_w = (N_HEADS + 2 * N_KV_HEADS) * HEAD_DIM
    kv_w = N_KV_HEADS * HEAD_DIM
    buckets, in_band = _band_tables()
    qg = jnp.tile(q_gain, N_HEADS).reshape(1, D_MODEL)
    kg = jnp.tile(k_gain, N_KV_HEADS).reshape(1, kv_w)
    smem = pl.BlockSpec(memory_space=pltpu.SMEM)
    return pl.pallas_call(
        _attn_body,
        grid=(bsz, s // ts),
        in_specs=[
            smem,
            smem,
            pl.BlockSpec((None, ts, D_MODEL), lambda b, i: (b, i, 0)),
            _const_spec((1, D_MODEL)),
            _const_spec((D_MODEL, qkv_w)),
            _const_spec((1, qkv_w)),
            _const_spec((1, D_MODEL)),
            _const_spec((1, kv_w)),
            _const_spec((BLOCK, 2 * BLOCK)),
            _const_spec((BLOCK, 2 * BLOCK)),
            _const_spec((D_MODEL, D_MODEL)),
            _const_spec((1, D_MODEL)),
        ],
        out_specs=pl.BlockSpec((None, ts, D_MODEL), lambda b, i: (b, i, 0)),
        out_shape=jax.ShapeDtypeStruct(x.shape, F32),
        scratch_shapes=[
            pltpu.VMEM((2, N_KV_HEADS, PAIRS_PER_KV * BLOCK, 4 * BLOCK), F32),
            pltpu.VMEM((ts, D_MODEL), BF16),
            pltpu.VMEM((2 * N_KV_HEADS, ts + BLOCK, kv_w), BF16),
            pltpu.VMEM((2 * N_KV_HEADS, ts + BLOCK, kv_w), BF16),
            pltpu.VMEM((ts, D_MODEL), BF16),
        ],
        compiler_params=_compiler_params(2),
        name="attn_mixer",
    )(sinks, rel_bias, x, g.reshape(1, D_MODEL), w_qkv.astype(BF16), b_qkv.reshape(1, qkv_w), qg, kg,
      jnp.asarray(buckets), jnp.asarray(in_band), w_o.astype(BF16), b_o.reshape(1, D_MODEL))


def kernel(x, ffn1_norm, ffn1_w_in, ffn1_w_out, mix_norm, ffn2_norm, ffn2_w_in, ffn2_w_out, conv_w_in, conv_w,
           conv_w_out, pool_w, pool_b, pool_scale, attn_w_qkv, attn_b_qkv, attn_q_norm, attn_k_norm, attn_sinks,
           attn_w_o, attn_b_o, rel_bias):
    bsz, s, d = x.shape
    ffn1 = (ffn1_norm.reshape(DEPTH, 1, d), ffn1_w_in, ffn1_w_out)
    ffn2 = (ffn2_norm.reshape(DEPTH, 1, d), ffn2_w_in, ffn2_w_out)

    def ffn(xx, params, layer):
        return _ffn(xx.reshape(bsz * s, d), *params, layer).reshape(bsz, s, d)

    for i in range(DEPTH):
        x = ffn(x, ffn1, i)
        kind, j = i % N_MIXERS, i // N_MIXERS
        if kind == 0:
            x = _conv_mixer(x, mix_norm[i], conv_w_in, conv_w, conv_w_out, j)
        elif kind == 1:
            x = _pool_mixer(x, mix_norm[i], pool_w[j], pool_b[j], pool_scale[j])
        else:
            x = _attn_mixer(x, mix_norm[i], attn_w_qkv[j], attn_b_qkv[j], attn_q_norm[j], attn_k_norm[j],
                            attn_sinks[j], attn_w_o[j], attn_b_o[j], rel_bias)
        x = ffn(x, ffn2, i)
    return x
```

```python
import functools
import math

import numpy as np
import jax
import jax.numpy as jnp
from jax import lax
from jax.experimental import pallas as pl
from jax.experimental.pallas import tpu as pltpu

D_MODEL = 1024
D_FF = 2816
DEPTH = 4
N_MIXERS = 3
CONV_WIDTH = 3
POOL_WINDOWS = (2, 4, 8, 16)
POOL_CH = D_MODEL // len(POOL_WINDOWS)
HEAD_DIM = 64
N_HEADS = D_MODEL // HEAD_DIM
N_KV_HEADS = 2
GQA_GROUP = N_HEADS // N_KV_HEADS
WINDOW = 128
BLOCK = 128
NUM_BUCKETS = 32
MAX_EXACT = NUM_BUCKETS // 2
MAX_DISTANCE = 128
EPS = 1e-6

LANES = 128
SUBLANES = 8
MASK_VALUE = -1e30
VMEM_LIMIT_BYTES = 56 * 1024 * 1024

FFN_TOKENS = 1024
FFN_CHUNK = 256
FFN_SUB = FFN_TOKENS // 2
FFN_PIECE = 64
FETCH_SLOTS = 4
FFN_WIN_ROWS = 64
FFN_WOUT_ROWS = 128
MIX_TOKENS = 1024
Q_BLOCKS = 4
CONV_TOKENS = 1024
CONV_SUB = 512
CONV_W_ROWS = 128
POOL_HALO = 16
HEAD_PAIRS = N_HEADS // 2
PAIRS_PER_KV = HEAD_PAIRS // N_KV_HEADS

F32 = jnp.float32
BF16 = jnp.bfloat16


def _rmsnorm_bf16(x, g):
    ms = jnp.mean(x * x, axis=-1, keepdims=True)
    return (x * lax.rsqrt(ms + EPS) * g).astype(BF16)


def _dot(a, b):
    return jnp.dot(a, b, preferred_element_type=F32)


def _const_spec(shape):
    return pl.BlockSpec(shape, lambda *_: (0,) * len(shape), pipeline_mode=pl.Buffered(1))


def _layer_spec(shape, layer):
    return pl.BlockSpec((None,) + shape, lambda *_: (layer,) + (0,) * len(shape), pipeline_mode=pl.Buffered(1))


def _fetch_weights_bf16(streams):
    def copy(stream, k):
        src, _, stage_ref, sem = stream
        slots, rows = stage_ref.shape[0], stage_ref.shape[1]
        return pltpu.make_async_copy(src.at[pl.ds(k * rows, rows)], stage_ref.at[k % slots], sem.at[k % slots])

    counts = []
    for stream in streams:
        src, _, stage_ref, _ = stream
        slots, rows = stage_ref.shape[0], stage_ref.shape[1]
        assert src.shape[0] % rows == 0
        counts.append(src.shape[0] // rows)
        for k in range(min(slots, counts[-1])):
            copy(stream, k).start()
    for k in range(max(counts)):
        for stream, n in zip(streams, counts):
            if k < n:
                _, dst_ref, stage_ref, _ = stream
                slots, rows = stage_ref.shape[0], stage_ref.shape[1]
                copy(stream, k).wait()
                dst_ref[k * rows:(k + 1) * rows, :] = stage_ref[k % slots].astype(BF16)
                if k + slots < n:
                    copy(stream, k + slots).start()


def _compiler_params(n_grid_dims):
    return pltpu.CompilerParams(
        dimension_semantics=("arbitrary",) * n_grid_dims,
        vmem_limit_bytes=VMEM_LIMIT_BYTES,
    )


def _ffn_body(x_ref, xn_ref, g_ref, win_hbm, wout_hbm, o_ref, h_ref, win_ref, wout_ref, stage_in, stage_out, sem_in,
              sem_out, *, layer):
    g = g_ref[...]

    @pl.when(pl.program_id(0) == 0)
    def _():
        _fetch_weights_bf16([(win_hbm.at[layer], win_ref, stage_in, sem_in),
                             (wout_hbm.at[layer], wout_ref, stage_out, sem_out)])
        h_ref[0] = _rmsnorm_bf16(x_ref[0:FFN_SUB, :], g)

    def norm_piece(src_ref, src_row, slot, k):
        r = k * FFN_PIECE
        hk = _rmsnorm_bf16(src_ref[src_row + r:src_row + r + FFN_PIECE, :], g)
        h_ref[slot, r:r + FFN_PIECE, :] = hk
        u = pltpu.bitcast(hk, jnp.uint32)
        t = u[0:SUBLANES]
        for rr in range(SUBLANES, u.shape[0], SUBLANES):
            t = t | u[rr:rr + SUBLANES]
        z = t[:, 0:LANES]
        for cc in range(LANES, D_MODEL, LANES):
            z = z | t[:, cc:cc + LANES]
        return (z >> 16) >> 16

    def attach(a, z):
        head = pltpu.bitcast(pltpu.bitcast(a[0:SUBLANES, 0:LANES], jnp.uint32) | z, F32)
        top = jnp.concatenate([head, a[0:SUBLANES, LANES:]], axis=1)
        return jnp.concatenate([top, a[SUBLANES:]], axis=0)

    for half in range(2):
        slot, other = half, 1 - half
        src_ref, src_row = (x_ref, FFN_SUB) if half == 0 else (xn_ref, 0)
        acc = None
        for c in range(D_FF // FFN_CHUNK):
            lo = c * FFN_CHUNK
            gate = _dot(h_ref[slot], win_ref[:, lo:lo + FFN_CHUNK])
            up = _dot(h_ref[slot], win_ref[:, D_FF + lo:D_FF + lo + FFN_CHUNK])
            if 1 <= c <= FFN_SUB // FFN_PIECE:
                up = attach(up, norm_piece(src_ref, src_row, other, c - 1))
            act = (gate * jax.nn.sigmoid(gate) * up).astype(BF16)
            part = _dot(act, wout_ref[lo:lo + FFN_CHUNK, :])
            acc = part if acc is None else acc + part
        rows = slice(half * FFN_SUB, (half + 1) * FFN_SUB)
        o_ref[rows, :] = x_ref[rows, :] + 0.5 * acc


def _ffn(x2, g, w_in, w_out, layer):
    t = x2.shape[0]
    assert t % FFN_TOKENS == 0
    n_sub = t // FFN_SUB
    return pl.pallas_call(
        functools.partial(_ffn_body, layer=layer),
        grid=(t // FFN_TOKENS,),
        in_specs=[
            pl.BlockSpec((FFN_TOKENS, D_MODEL), lambda i: (i, 0)),
            pl.BlockSpec((FFN_SUB, D_MODEL), lambda i: (jnp.minimum(2 * i + 2, n_sub - 1), 0)),
            _layer_spec((1, D_MODEL), layer),
            pl.BlockSpec(memory_space=pl.ANY),
            pl.BlockSpec(memory_space=pl.ANY),
        ],
        out_specs=pl.BlockSpec((FFN_TOKENS, D_MODEL), lambda i: (i, 0)),
        out_shape=jax.ShapeDtypeStruct((t, D_MODEL), F32),
        scratch_shapes=[
            pltpu.VMEM((2, FFN_SUB, D_MODEL), BF16),
            pltpu.VMEM((D_MODEL, 2 * D_FF), BF16),
            pltpu.VMEM((D_FF, D_MODEL), BF16),
            pltpu.VMEM((FETCH_SLOTS, FFN_WIN_ROWS, 2 * D_FF), F32),
            pltpu.VMEM((FETCH_SLOTS, FFN_WOUT_ROWS, D_MODEL), F32),
            pltpu.SemaphoreType.DMA((FETCH_SLOTS,)),
            pltpu.SemaphoreType.DMA((FETCH_SLOTS,)),
        ],
        compiler_params=_compiler_params(1),
        name="ffn",
    )(x2, x2, g, w_in, w_out)


def _conv_body(x_ref, g_ref, win_hbm, cw_ref, wout_hbm, o_ref, prev_ref, win_ref, wout_ref, stage_in, stage_out,
               sem_in, sem_out, *, layer):
    ts = x_ref.shape[0]

    @pl.when(jnp.logical_and(pl.program_id(0) == 0, pl.program_id(1) == 0))
    def _():
        _fetch_weights_bf16([(win_hbm.at[layer], win_ref, stage_in, sem_in),
                             (wout_hbm.at[layer], wout_ref, stage_out, sem_out)])

    @pl.when(pl.program_id(1) == 0)
    def _():
        prev_ref[...] = jnp.zeros_like(prev_ref)

    w0 = cw_ref[0:1, :]
    w1 = cw_ref[1:2, :]
    w2 = cw_ref[2:3, :]
    row = lax.broadcasted_iota(jnp.int32, (SUBLANES, D_MODEL), 0)
    prev = prev_ref[...]
    for r in range(0, ts, CONV_SUB):
        x = x_ref[r:r + CONV_SUB, :]
        h = _rmsnorm_bf16(x, g_ref[...])
        b = _dot(h, win_ref[:, 0:D_MODEL])
        c = _dot(h, win_ref[:, D_MODEL:2 * D_MODEL])
        v = _dot(h, win_ref[:, 2 * D_MODEL:3 * D_MODEL])
        z = c * v
        z1 = pltpu.roll(z, 1, 0)
        z2 = pltpu.roll(z, 2, 0)
        head1 = jnp.where(row < 1, pltpu.roll(prev, 1, 0), z1[:SUBLANES])
        head2 = jnp.where(row < 2, pltpu.roll(prev, 2, 0), z2[:SUBLANES])
        conv_head = w0 * head2 + w1 * head1 + w2 * z[:SUBLANES]
        conv_rest = w0 * z2[SUBLANES:] + w1 * z1[SUBLANES:] + w2 * z[SUBLANES:]
        conv = jnp.concatenate([conv_head, conv_rest], axis=0)
        prev = z[CONV_SUB - SUBLANES:, :]

        u = (b * conv).astype(BF16)
        o_ref[r:r + CONV_SUB, :] = x + _dot(u, wout_ref[...])
    prev_ref[...] = prev


def _conv_mixer(x, g, w_in, conv_w, w_out, layer):
    bsz, s, _ = x.shape
    ts = CONV_TOKENS
    assert s % ts == 0
    return pl.pallas_call(
        functools.partial(_conv_body, layer=layer),
        grid=(bsz, s // ts),
        in_specs=[
            pl.BlockSpec((None, ts, D_MODEL), lambda b, i: (b, i, 0)),
            _const_spec((1, D_MODEL)),
            pl.BlockSpec(memory_space=pl.ANY),
            _layer_spec((CONV_WIDTH, D_MODEL), layer),
            pl.BlockSpec(memory_space=pl.ANY),
        ],
        out_specs=pl.BlockSpec((None, ts, D_MODEL), lambda b, i: (b, i, 0)),
        out_shape=jax.ShapeDtypeStruct(x.shape, F32),
        scratch_shapes=[
            pltpu.VMEM((SUBLANES, D_MODEL), F32),
            pltpu.VMEM((D_MODEL, 3 * D_MODEL), BF16),
            pltpu.VMEM((D_MODEL, D_MODEL), BF16),
            pltpu.VMEM((FETCH_SLOTS, CONV_W_ROWS, 3 * D_MODEL), F32),
            pltpu.VMEM((FETCH_SLOTS, CONV_W_ROWS, D_MODEL), F32),
            pltpu.SemaphoreType.DMA((FETCH_SLOTS,)),
            pltpu.SemaphoreType.DMA((FETCH_SLOTS,)),
        ],
        compiler_params=_compiler_params(2),
        name="conv_mixer",
    )(x, g.reshape(1, D_MODEL), w_in, conv_w, w_out)


def _pool_body(x_ref, g_ref, w_ref, b_ref, sc_ref, o_ref, prev_ref):
    ts = x_ref.shape[0]
    i = pl.program_id(1)

    @pl.when(i == 0)
    def _():
        prev_ref[...] = jnp.zeros_like(prev_ref)

    x = x_ref[...]
    ms = jnp.mean(x * x, axis=-1, keepdims=True)
    h = x * lax.rsqrt(ms + EPS) * g_ref[...]
    prev = prev_ref[...]
    prev_ref[...] = h[ts - POOL_HALO:, :]

    pos = i * ts + lax.broadcasted_iota(jnp.int32, (ts, POOL_CH), 0)
    ys = []
    for gi, win in enumerate(POOL_WINDOWS):
        lo = gi * POOL_CH
        hg = h[:, lo:lo + POOL_CH]
        ssum = jnp.concatenate([prev[:, lo:lo + POOL_CH], hg], axis=0)
        shift = 1
        while shift < win:
            ssum = ssum + pltpu.roll(ssum, shift, 0)
            shift *= 2
        cnt = jnp.minimum(pos + 1, win).astype(F32)
        diff = ssum[POOL_HALO:] / cnt - hg
        ys.append(_dot(diff.astype(BF16), w_ref[gi]))
    y = jnp.concatenate(ys, axis=1)
    o_ref[...] = x + (y + b_ref[...]) * sc_ref[...]


def _pool_mixer(x, g, w, bias, scale):
    bsz, s, _ = x.shape
    ts = CONV_TOKENS
    assert s % ts == 0
    ng = len(POOL_WINDOWS)
    return pl.pallas_call(
        _pool_body,
        grid=(bsz, s // ts),
        in_specs=[
            pl.BlockSpec((None, ts, D_MODEL), lambda b, i: (b, i, 0)),
            _const_spec((1, D_MODEL)),
            _const_spec((ng, POOL_CH, POOL_CH)),
            _const_spec((1, D_MODEL)),
            _const_spec((1, D_MODEL)),
        ],
        out_specs=pl.BlockSpec((None, ts, D_MODEL), lambda b, i: (b, i, 0)),
        out_shape=jax.ShapeDtypeStruct(x.shape, F32),
        scratch_shapes=[pltpu.VMEM((POOL_HALO, D_MODEL), F32)],
        compiler_params=_compiler_params(2),
        name="pool_mixer",
    )(x, g.reshape(1, D_MODEL), w.astype(BF16), bias.reshape(1, D_MODEL), scale.reshape(1, D_MODEL))


def _t5_causal_buckets(n):
    nf = np.maximum(n, 1).astype(np.float32)
    large = MAX_EXACT + (np.log(nf / MAX_EXACT) / math.log(MAX_DISTANCE / MAX_EXACT)
                         * (NUM_BUCKETS - MAX_EXACT)).astype(np.int32)
    large = np.minimum(large, NUM_BUCKETS - 1)
    return np.where(n < MAX_EXACT, n, large).astype(np.int32)


def _band_tables():
    qi = np.arange(BLOCK)[:, None]
    ki = np.arange(2 * BLOCK)[None, :]
    dist = qi + BLOCK - ki
    in_band = ((dist >= 0) & (dist < WINDOW)).astype(np.int32)
    assert not in_band[:, 0].any()
    return _t5_causal_buckets(dist), in_band


def _group_sumsq(x, ones_blk):
    sq = x * x
    hi = sq.astype(BF16)
    lo = (sq - hi.astype(F32)).astype(BF16)
    n = x.shape[1]
    w = ones_blk.shape[0]
    outs = [_dot(hi[:, s:s + w], ones_blk) + _dot(lo[:, s:s + w], ones_blk) for s in range(0, n, w)]
    return outs[0] if len(outs) == 1 else jnp.concatenate(outs, axis=1)


def _attn_body(sink_ref, relb_ref, x_ref, g_ref, wqkv_ref, bqkv_ref, qg_ref, kg_ref, bkt_ref, band_ref,
               wo_ref, bo_ref, o_ref, bias_ref, q_ref, kext_ref, vext_ref, oacc_ref):
    ts = x_ref.shape[0]
    nblk = ts // BLOCK
    first_tile = pl.program_id(1) == 0
    kv_w = N_KV_HEADS * HEAD_DIM

    @pl.when(jnp.logical_and(pl.program_id(0) == 0, pl.program_id(1) == 0))
    def _():
        bkt = bkt_ref[...]
        band = band_ref[...] > 0
        key = lax.broadcasted_iota(jnp.int32, (BLOCK, 2 * BLOCK), 1)
        has_prev = key >= BLOCK
        sink_slot = key == 0
        for kv in range(N_KV_HEADS):
            for pair in range(PAIRS_PER_KV):
                for par in range(2):
                    head = kv * GQA_GROUP + 2 * pair + par

                    def pick(b, acc):
                        return jnp.where(bkt == b, relb_ref[b, head], acc)

                    vals = lax.fori_loop(0, NUM_BUCKETS, pick, jnp.zeros((BLOCK, 2 * BLOCK), F32))
                    vals = jnp.where(band, vals, MASK_VALUE)
                    sink = sink_ref[head]
                    rows = slice(pair * BLOCK, (pair + 1) * BLOCK)
                    cols = slice(par * 2 * BLOCK, (par + 1) * 2 * BLOCK)
                    bias_ref[0, kv, rows, cols] = jnp.where(sink_slot, sink, vals)
                    bias_ref[1, kv, rows, cols] = jnp.where(sink_slot, sink, jnp.where(has_prev, vals, MASK_VALUE))

    @pl.when(first_tile)
    def _():
        kext_ref[:, 0:BLOCK, :] = jnp.zeros((2 * N_KV_HEADS, BLOCK, kv_w), BF16)
        low_prev = lax.broadcasted_iota(jnp.int32, (BLOCK, kv_w), 1) < HEAD_DIM
        for c in range(N_KV_HEADS):
            vext_ref[2 * c, 0:BLOCK, :] = jnp.where(low_prev, 0.0, 1.0).astype(BF16)
            vext_ref[2 * c + 1, 0:BLOCK, :] = jnp.where(low_prev, 1.0, 0.0).astype(BF16)

    x = x_ref[...]
    h = _rmsnorm_bf16(x, g_ref[...])

    head_shift = HEAD_DIM.bit_length() - 1
    r4 = lax.broadcasted_iota(jnp.int32, (2 * LANES, 2 * LANES), 0) >> head_shift
    c4 = lax.broadcasted_iota(jnp.int32, (2 * LANES, 2 * LANES), 1) >> head_shift
    ones4 = jnp.where(r4 == c4, 1.0, 0.0).astype(BF16)

    qrows = ts // Q_BLOCKS

    def q_dot(b):
        return _dot(h[b * qrows:(b + 1) * qrows], wqkv_ref[:, 0:D_MODEL]) + bqkv_ref[:, 0:D_MODEL]

    def q_store(b, q):
        qss = _group_sumsq(q, ones4)
        q_ref[b * qrows:(b + 1) * qrows, :] = (q * lax.rsqrt(qss + HEAD_DIM * EPS) * qg_ref[...]).astype(BF16)

    q0 = q_dot(0)
    kvp = _dot(h, wqkv_ref[:, D_MODEL:D_MODEL + 2 * kv_w]) + bqkv_ref[:, D_MODEL:D_MODEL + 2 * kv_w]
    q1 = q_dot(1)
    k = kvp[:, 0:kv_w]
    v = kvp[:, kv_w:2 * kv_w]
    ones2 = ones4[0:kv_w, 0:kv_w]
    kss = _group_sumsq(k, ones2)
    q_store(0, q0)
    q2 = q_dot(2)
    kn = k * lax.rsqrt(kss * (1.0 / HEAD_DIM) + EPS) * kg_ref[...]

    lane = lax.broadcasted_iota(jnp.int32, (ts, kv_w), 1)
    low = lane < HEAD_DIM
    kn_sw = pltpu.roll(kn, HEAD_DIM, 1)
    v_sw = pltpu.roll(v, HEAD_DIM, 1)
    zkv = None
    for c in range(N_KV_HEADS):
        for p in range(2):
            half = low if p == 0 else jnp.logical_not(low)
            ksrc = kn if c == p else kn_sw
            vsrc = v if c == p else v_sw
            for ref, staged in ((kext_ref, jnp.where(half, ksrc, 0.0).astype(BF16)),
                                (vext_ref, jnp.where(half, vsrc, 1.0).astype(BF16))):
                ref[2 * c + p, BLOCK:, :] = staged
                u = pltpu.bitcast(staged, jnp.uint32)
                t = u[0:SUBLANES]
                for rr in range(SUBLANES, u.shape[0], SUBLANES):
                    t = t | u[rr:rr + SUBLANES]
                zkv = t if zkv is None else zkv | t
    zkv = (zkv >> 16) >> 16
    q_store(1, q1)
    q3 = q_dot(3)
    q_store(2, q2)
    head = pltpu.bitcast(pltpu.bitcast(q3[0:SUBLANES, 0:LANES], jnp.uint32) | zkv, F32)
    q3 = jnp.concatenate([jnp.concatenate([head, q3[0:SUBLANES, LANES:]], axis=1), q3[SUBLANES:]], axis=0)
    q_store(3, q3)

    low_blk = lax.broadcasted_iota(jnp.int32, (BLOCK, LANES), 1) < HEAD_DIM
    bf16_rows = 2 * SUBLANES
    slot0 = lax.broadcasted_iota(jnp.int32, (bf16_rows, LANES), 0) == 0
    low0 = lax.broadcasted_iota(jnp.int32, (bf16_rows, LANES), 1) < HEAD_DIM
    zeros0 = jnp.zeros((bf16_rows, LANES), BF16)

    def clear_slot0(a, mask):
        return jnp.concatenate([jnp.where(mask, zeros0, a[:bf16_rows]), a[bf16_rows:]], axis=0)

    for j in range(nblk):
        r0 = j * BLOCK
        variant = jnp.where(first_tile, 1, 0) if j == 0 else 0
        for c in range(N_KV_HEADS):
            qs = jnp.concatenate(
                [q_ref[r0:r0 + BLOCK, (c * PAIRS_PER_KV + i) * LANES:(c * PAIRS_PER_KV + i + 1) * LANES]
                 for i in range(PAIRS_PER_KV)], axis=0)
            ox = []
            for p in range(2):
                v_half = low0 if p == 0 else jnp.logical_not(low0)
                kx = clear_slot0(kext_ref[2 * c + p, r0:r0 + 2 * BLOCK, :], slot0)
                vx = clear_slot0(vext_ref[2 * c + p, r0:r0 + 2 * BLOCK, :], jnp.logical_and(slot0, v_half))
                s = lax.dot_general(qs, kx, (((1,), (1,)), ((), ())), preferred_element_type=F32)
                s = s + bias_ref[variant, c, :, p * 2 * BLOCK:(p + 1) * 2 * BLOCK]
                pes = []
                for i in range(PAIRS_PER_KV):
                    si = s[i * BLOCK:(i + 1) * BLOCK]
                    m = jnp.max(si, axis=-1, keepdims=True)
                    pes.append(jnp.exp(si - m).astype(BF16))
                ox.append(_dot(jnp.concatenate(pes, axis=0), vx))
            for i in range(PAIRS_PER_KV):
                pair = c * PAIRS_PER_KV + i
                ox_e = ox[0][i * BLOCK:(i + 1) * BLOCK]
                ox_o = ox[1][i * BLOCK:(i + 1) * BLOCK]
                num = jnp.where(low_blk, ox_e, ox_o)
                den = pltpu.roll(jnp.where(low_blk, ox_o, ox_e), HEAD_DIM, 1)
                oacc_ref[r0:r0 + BLOCK, pair * LANES:(pair + 1) * LANES] = (num / den).astype(BF16)

    kext_ref[:, 0:BLOCK, :] = kext_ref[:, ts:ts + BLOCK, :]
    vext_ref[:, 0:BLOCK, :] = vext_ref[:, ts:ts + BLOCK, :]

    o_ref[...] = x + _dot(oacc_ref[...], wo_ref[...]) + bo_ref[...]


def _attn_mixer(x, g, w_qkv, b_qkv, q_gain, k_gain, sinks, w_o, b_o, rel_bias):
    bsz, s, _ = x.shape
    ts = MIX_TOKENS
    assert s % ts == 0 and ts % BLOCK == 0
    qkv_w = (N_HEADS + 2 * N_KV_HEADS) * HEAD_DIM
    kv_w = N_KV_HEADS * HEAD_DIM
    buckets, in_band = _band_tables()
    qg = jnp.tile(q_gain, N_HEADS).reshape(1, D_MODEL)
    kg = jnp.tile(k_gain, N_KV_HEADS).reshape(1, kv_w)
    smem = pl.BlockSpec(memory_space=pltpu.SMEM)
    return pl.pallas_call(
        _attn_body,
        grid=(bsz, s // ts),
        in_specs=[
            smem,
            smem,
            pl.BlockSpec((None, ts, D_MODEL), lambda b, i: (b, i, 0)),
            _const_spec((1, D_MODEL)),
            _const_spec((D_MODEL, qkv_w)),
            _const_spec((1, qkv_w)),
            _const_spec((1, D_MODEL)),
            _const_spec((1, kv_w)),
            _const_spec((BLOCK, 2 * BLOCK)),
            _const_spec((BLOCK, 2 * BLOCK)),
            _const_spec((D_MODEL, D_MODEL)),
            _const_spec((1, D_MODEL)),
        ],
        out_specs=pl.BlockSpec((None, ts, D_MODEL), lambda b, i: (b, i, 0)),
        out_shape=jax.ShapeDtypeStruct(x.shape, F32),
        scratch_shapes=[
            pltpu.VMEM((2, N_KV_HEADS, PAIRS_PER_KV * BLOCK, 4 * BLOCK), F32),
            pltpu.VMEM((ts, D_MODEL), BF16),
            pltpu.VMEM((2 * N_KV_HEADS, ts + BLOCK, kv_w), BF16),
            pltpu.VMEM((2 * N_KV_HEADS, ts + BLOCK, kv_w), BF16),
            pltpu.VMEM((ts, D_MODEL), BF16),
        ],
        compiler_params=_compiler_params(2),
        name="attn_mixer",
    )(sinks, rel_bias, x, g.reshape(1, D_MODEL), w_qkv.astype(BF16), b_qkv.reshape(1, qkv_w), qg, kg,
      jnp.asarray(buckets), jnp.asarray(in_band), w_o.astype(BF16), b_o.reshape(1, D_MODEL))


def kernel(x, ffn1_norm, ffn1_w_in, ffn1_w_out, mix_norm, ffn2_norm, ffn2_w_in, ffn2_w_out, conv_w_in, conv_w,
           conv_w_out, pool_w, pool_b, pool_scale, attn_w_qkv, attn_b_qkv, attn_q_norm, attn_k_norm, attn_sinks,
           attn_w_o, attn_b_o, rel_bias):
    bsz, s, d = x.shape
    ffn1 = (ffn1_norm.reshape(DEPTH, 1, d), ffn1_w_in, ffn1_w_out)
    ffn2 = (ffn2_norm.reshape(DEPTH, 1, d), ffn2_w_in, ffn2_w_out)

    def ffn(xx, params, layer):
        return _ffn(xx.reshape(bsz * s, d), *params, layer).reshape(bsz, s, d)

    for i in range(DEPTH):
        x = ffn(x, ffn1, i)
        kind, j = i % N_MIXERS, i // N_MIXERS
        if kind == 0:
            x = _conv_mixer(x, mix_norm[i], conv_w_in, conv_w, conv_w_out, j)
        elif kind == 1:
            x = _pool_mixer(x, mix_norm[i], pool_w[j], pool_b[j], pool_scale[j])
        else:
            x = _attn_mixer(x, mix_norm[i], attn_w_qkv[j], attn_b_qkv[j], attn_q_norm[j], attn_k_norm[j],
                            attn_sinks[j], attn_w_o[j], attn_b_o[j], rel_bias)
        x = ffn(x, ffn2, i)
    return x
```
